```python
import jax, jax.numpy as jnp
from jax import lax
import numpy as np

D_MODEL = 4096
BATCH = 16
SEQ = 256
DEPTH = 4
DEC_BATCH = 2
DEC_SEQ = 1024
PAST_LEN = 512

GRID_W = 64
Q_BLOCK = 128
WINDOW = 128
ROPE_BASE = 10000.0
NORM_EPS = 1e-6
NEG_INF = -1e30
N_EVEN = (DEPTH + 1) // 2
N_ODD = DEPTH // 2

HEAD_DIM = 128
A_HEADS = 16
A_KV_HEADS = 4
A_GROUP = A_HEADS // A_KV_HEADS
A_Q_W = A_HEADS * HEAD_DIM
A_KV_W = A_KV_HEADS * HEAD_DIM
B_WIDTH = D_MODEL // 2
EVEN_IN = A_Q_W + 2 * A_KV_W + 3 * B_WIDTH
EVEN_OUT = A_Q_W + B_WIDTH
C_HEADS = 16
Q_LORA = D_MODEL // 4
KV_LORA = D_MODEL // 8
QK_NOPE = 128
QK_ROPE = 64
V_DIM = 128
C_OUT_W = C_HEADS * V_DIM
D_WIDTH = D_MODEL // 2
D_HEAD_SIZE = 64
D_HEADS = D_WIDTH // D_HEAD_SIZE
DECAY_LORA = 64
AAA_LORA = 64
GATE_LORA = 256
GN_EPS = 64e-5
SHIFT_W = 3 * D_WIDTH + DECAY_LORA + AAA_LORA
ODD_IN = Q_LORA + KV_LORA + QK_ROPE + SHIFT_W + GATE_LORA
ODD_OUT = C_OUT_W + D_WIDTH
N_GROUPS = 8
EXPERTS_PER_GROUP = 8
N_EXPERTS = N_GROUPS * EXPERTS_PER_GROUP
TOP_K = 2
EXPERT_FF = 512
MOE_BLOCK = 128

kernel_name = 'hybrid_diffusion_prefix_trunk_step'


def _rmsnorm(x, g):
    xf = x.astype(jnp.float32)
    y = xf * lax.rsqrt(jnp.mean(xf * xf, axis=-1, keepdims=True) + NORM_EPS)
    return y.astype(x.dtype) * g


def _modulate(x, g, shift, scale):
    return _rmsnorm(x, g) * (1 + scale) + shift


def _modulation(cvec, w_mod, b_mod):
    return jnp.split(jax.nn.silu(cvec) @ w_mod + b_mod, 6, axis=-1)


def _axial_rope_tables(n_tok, rot_dim):
    n_rows = n_tok // GRID_W
    t = jnp.arange(n_rows * GRID_W)
    rows = (t // GRID_W).astype(jnp.float32)
    cols = (t % GRID_W).astype(jnp.float32)
    axis_dim = rot_dim // 2
    inv = ROPE_BASE ** (-jnp.arange(0, axis_dim, 2, dtype=jnp.float32) / axis_dim)
    ang_r = rows[:, None] * inv[None, :]
    ang_c = cols[:, None] * inv[None, :]
    return jnp.cos(ang_r), jnp.sin(ang_r), jnp.cos(ang_c), jnp.sin(ang_c)


def _rope_rotate(x, cos, sin):
    half = x.shape[-1] // 2
    shape = (1, cos.shape[0]) + (1,) * (x.ndim - 3) + (half,)
    c, s = cos.reshape(shape), sin.reshape(shape)
    x1, x2 = x[..., :half], x[..., half:]
    return jnp.concatenate([x1 * c - x2 * s, x2 * c + x1 * s], axis=-1).astype(x.dtype)


def _apply_axial_rope(x, tables):
    cos_r, sin_r, cos_c, sin_c = tables
    h = x.shape[-1] // 2
    return jnp.concatenate([_rope_rotate(x[..., :h], cos_r, sin_r),
                            _rope_rotate(x[..., h:], cos_c, sin_c)], axis=-1)


def _to_blocks(x):
    b, t = x.shape[:2]
    return jnp.moveaxis(x.reshape((b, t // Q_BLOCK, Q_BLOCK) + x.shape[2:]), 1, 0)


def _from_blocks(y):
    y = jnp.moveaxis(y, 0, 1)
    return y.reshape((y.shape[0], y.shape[1] * y.shape[2]) + y.shape[3:])


def _gqa_scores(q, k):
    return jnp.einsum('bqhgd,bkhd->bhgqk', q, k, preferred_element_type=jnp.float32)


def _joint_softmax_read(scores, values, sink=None):
    s = jnp.concatenate(scores, axis=-1)
    if sink is not None:
        sk = jnp.broadcast_to(sink.astype(jnp.float32)[None, :, :, None, None], s.shape[:-1] + (1,))
        s = jnp.concatenate([s, sk], axis=-1)
    p = jax.nn.softmax(s, axis=-1)
    if sink is not None:
        p = p[..., :-1]
    v = jnp.concatenate(values, axis=1)
    return jnp.einsum('bhgqk,bkhd->bqhgd', p.astype(v.dtype), v)


def _even_streams(h, w_in):
    b, t = h.shape[:2]
    p = h @ w_in
    o1 = A_Q_W
    o2 = o1 + A_KV_W
    o3 = o2 + A_KV_W
    o4 = o3 + B_WIDTH
    o5 = o4 + B_WIDTH
    q, k, v, bg, cg, u = jnp.split(p, [o1, o2, o3, o4, o5], axis=-1)
    q = q.reshape(b, t, A_KV_HEADS, A_GROUP, HEAD_DIM)
    k = k.reshape(b, t, A_KV_HEADS, HEAD_DIM)
    v = v.reshape(b, t, A_KV_HEADS, HEAD_DIM)
    return q, k, v, bg, cg, u


def _short_conv(bg, cg, u, conv_w):
    z = cg * u
    zp = jnp.pad(z, ((0, 0), (1, 1), (0, 0)))
    conv = conv_w[0] * zp[:, :-2] + conv_w[1] * zp[:, 1:-1] + conv_w[2] * zp[:, 2:]
    return bg * conv


def _even_mixer_context(h, w_in, w_out, sink, conv_w):
    b, t = h.shape[:2]
    q, k, v, bg, cg, u = _even_streams(h, w_in)
    sink = sink.reshape(A_KV_HEADS, A_GROUP)
    scale = HEAD_DIM ** -0.5

    def block(qb):
        return _joint_softmax_read([_gqa_scores(qb, k) * scale], [v], sink)

    att = _from_blocks(lax.map(block, _to_blocks(q))).reshape(b, t, A_Q_W)
    y = jnp.concatenate([att, _short_conv(bg, cg, u, conv_w)], axis=-1) @ w_out
    return y, k, v


def _window_attention(q_rot, q_raw, k_rot, v, ck, cv, sink):
    b, n_tok = q_rot.shape[:2]
    n_blk = n_tok // Q_BLOCK
    scale = HEAD_DIM ** -0.5
    pad = jnp.zeros((b, Q_BLOCK) + k_rot.shape[2:], k_rot.dtype)
    kp = jnp.concatenate([pad, k_rot, pad], axis=1)
    vp = jnp.concatenate([pad.astype(v.dtype), v, pad.astype(v.dtype)], axis=1)
    rel = jnp.arange(3 * Q_BLOCK)[None, :] - Q_BLOCK - jnp.arange(Q_BLOCK)[:, None]
    band = jnp.abs(rel) <= WINDOW

    def block(args):
        i, qr, qw = args
        kb = lax.dynamic_slice_in_dim(kp, i * Q_BLOCK, 3 * Q_BLOCK, axis=1)
        vb = lax.dynamic_slice_in_dim(vp, i * Q_BLOCK, 3 * Q_BLOCK, axis=1)
        kpos = (i - 1) * Q_BLOCK + jnp.arange(3 * Q_BLOCK)
        valid = band & ((kpos >= 0) & (kpos < n_tok))[None, :]
        s_loc = jnp.where(valid, _gqa_scores(qr, kb) * scale, NEG_INF)
        s_ctx = _gqa_scores(qw, ck) * scale
        return _joint_softmax_read([s_loc, s_ctx], [vb, cv], sink)

    out = lax.map(block, (jnp.arange(n_blk), _to_blocks(q_rot), _to_blocks(q_raw)))
    return _from_blocks(out)


def _even_mixer_latent(h, ck, cv, w_in, w_out, sink, conv_w):
    b, t = h.shape[:2]
    q, k, v, bg, cg, u = _even_streams(h, w_in)
    tabs = _axial_rope_tables(t, HEAD_DIM)
    att = _window_attention(_apply_axial_rope(q, tabs), q, _apply_axial_rope(k, tabs), v,
                            ck, cv, sink.reshape(A_KV_HEADS, A_GROUP)).reshape(b, t, A_Q_W)
    return jnp.concatenate([att, _short_conv(bg, cg, u, conv_w)], axis=-1) @ w_out


def _odd_streams(h, w_in):
    p = h @ w_in
    o1 = Q_LORA
    o2 = o1 + KV_LORA
    o3 = o2 + QK_ROPE
    o4 = o3 + SHIFT_W
    q_down, kv_down, k_pe, rw, g_lo = jnp.split(p, [o1, o2, o3, o4], axis=-1)
    return q_down, kv_down, k_pe, rw, g_lo


def _mla_queries(q_down, kv_down, mp):
    b, t = q_down.shape[:2]
    q = (_rmsnorm(q_down, mp['q_norm']) @ mp['w_uq']).reshape(b, t, C_HEADS, QK_NOPE + QK_ROPE)
    c_kv = _rmsnorm(kv_down, mp['kv_norm'])
    return q[..., :QK_NOPE], q[..., QK_NOPE:], c_kv


def _mla_expand(c_kv, w_ukv):
    b, t = c_kv.shape[:2]
    kv = (c_kv @ w_ukv).reshape(b, t, C_HEADS, QK_NOPE + V_DIM)
    return kv[..., :QK_NOPE], kv[..., QK_NOPE:]


def _mla_keys(k_nope, k_pe):
    kpe = jnp.broadcast_to(k_pe[:, :, None, :], k_nope.shape[:3] + (QK_ROPE,)).astype(k_nope.dtype)
    return jnp.concatenate([k_nope, kpe], axis=-1)


def _shift_prev(z):
    return jnp.pad(z, ((0, 0), (1, 0), (0, 0)))[:, :-1]


def _shift_next(z):
    return jnp.pad(z, ((0, 0), (0, 1), (0, 0)))[:, 1:]


def _rwkv_direction(rw, shift, mu, w0, w_up, a0, a_up, k_k, k_a):
    b, t = rw.shape[:2]
    z = (rw + mu * (shift(rw) - rw)).astype(jnp.float32)
    r, k, v, w_lo, a_lo = jnp.split(z, [D_WIDTH, 2 * D_WIDTH, 3 * D_WIDTH, 3 * D_WIDTH + DECAY_LORA], axis=-1)
    w_raw = w0 + jnp.tanh(w_lo) @ w_up.astype(jnp.float32)
    decay = jnp.exp(-jnp.exp(-jax.nn.softplus(-w_raw) - 0.5))
    a = jax.nn.sigmoid(a0 + a_lo @ a_up.astype(jnp.float32))
    heads = lambda x: x.reshape(b, t, D_HEADS, D_HEAD_SIZE)
    kk = heads(k * k_k)
    kk = kk / jnp.maximum(jnp.sqrt(jnp.sum(kk * kk, axis=-1, keepdims=True)), 1e-12)
    k = k * (1 + (a - 1) * k_a)
    a_h = heads(a)
    return heads(r), heads(decay), heads(k), heads(v), -kk, kk * a_h


def _rwkv_scan(s0, r, w, k, v, a, b, reverse):
    def step(s, inp):
        r_t, w_t, k_t, v_t, a_t, b_t = inp
        sa = jnp.einsum('bhij,bhj->bhi', s, a_t)
        s = s * w_t[:, :, None, :] + sa[..., None] * b_t[:, :, None, :] + v_t[..., None] * k_t[:, :, None, :]
        return s, jnp.einsum('bhij,bhj->bhi', s, r_t)

    xs = tuple(jnp.moveaxis(x, 1, 0) for x in (r, w, k, v, a, b))
    s_fin, ys = lax.scan(step, s0, xs, reverse=reverse)
    return s_fin, jnp.moveaxis(ys, 0, 1)


def _rwkv_readout(y, r, k, v, r_k, ln_w, ln_b):
    b, t = y.shape[:2]
    mu = jnp.mean(y, axis=-1, keepdims=True)
    var = jnp.mean(jnp.square(y - mu), axis=-1, keepdims=True)
    yn = ((y - mu) * lax.rsqrt(var + GN_EPS)).reshape(b, t, D_WIDTH) * ln_w + ln_b
    bonus = (jnp.sum(r * k * r_k, axis=-1, keepdims=True) * v).reshape(b, t, D_WIDTH)
    return yn + bonus


def _rwkv_mixer(rw, g_lo, s0, rp):
    s0 = s0.astype(jnp.float32)
    outs, finals = [], []
    for d, (shift, rev) in enumerate(((_shift_prev, False), (_shift_next, True))):
        r, w, k, v, a_in, b_in = _rwkv_direction(rw, shift, rp['mu'][d], rp['w0'][d], rp['w_up'][d],
                                                 rp['a0'][d], rp['a_up'][d], rp['k_k'][d], rp['k_a'][d])
        s_fin, y = _rwkv_scan(s0[:, d], r, w, k, v, a_in, b_in, rev)
        outs.append(_rwkv_readout(y, r, k, v, rp['r_k'][d], rp['ln_w'], rp['ln_b']))
        finals.append(s_fin)
    g = jax.nn.sigmoid(g_lo) @ rp['g_up']
    return ((outs[0] + outs[1]) * g).astype(rw.dtype), jnp.stack(finals, axis=1)


def _odd_mixer_context(h, w_in, w_out, mp, rp):
    b, t = h.shape[:2]
    q_down, kv_down, k_pe, rw, g_lo = _odd_streams(h, w_in)
    q_nope, q_pe, c_kv = _mla_queries(q_down, kv_down, mp)
    k_nope, v = _mla_expand(c_kv, mp['w_ukv'])
    q = jnp.concatenate([q_nope, q_pe], axis=-1)[:, :, :, None, :]
    k = _mla_keys(k_nope, k_pe)
    scale = (QK_NOPE + QK_ROPE) ** -0.5

    def block(qb):
        return _joint_softmax_read([_gqa_scores(qb, k) * scale], [v])

    att = _from_blocks(lax.map(block, _to_blocks(q))).reshape(b, t, C_OUT_W)
    s0 = jnp.zeros((b, 2, D_HEADS, D_HEAD_SIZE, D_HEAD_SIZE), jnp.float32)
    rwo, s_fin = _rwkv_mixer(rw, g_lo, s0, rp)
    y = jnp.concatenate([att, rwo.astype(att.dtype)], axis=-1) @ w_out
    return y, c_kv, k_pe, s_fin


def _odd_mixer_latent(h, ckv_ctx, kpe_ctx, s_ctx, w_in, w_out, mp, rp):
    b, t = h.shape[:2]
    q_down, kv_down, k_pe, rw, g_lo = _odd_streams(h, w_in)
    q_nope, q_pe, c_kv = _mla_queries(q_down, kv_down, mp)
    k_nope, v = _mla_expand(c_kv, mp['w_ukv'])
    tabs = _axial_rope_tables(t, QK_ROPE)
    q_rot = jnp.concatenate([q_nope, _apply_axial_rope(q_pe, tabs)], axis=-1)[:, :, :, None, :]
    q_raw = jnp.concatenate([q_nope, q_pe], axis=-1)[:, :, :, None, :]
    k_lat = _mla_keys(k_nope, _apply_axial_rope(k_pe[:, :, None, :], tabs)[:, :, 0])
    ck_nope, cv = _mla_expand(ckv_ctx, mp['w_ukv'])
    k_ctx = _mla_keys(ck_nope, kpe_ctx)
    scale = (QK_NOPE + QK_ROPE) ** -0.5

    def block(args):
        qr, qw = args
        return _joint_softmax_read([_gqa_scores(qr, k_lat) * scale, _gqa_scores(qw, k_ctx) * scale], [v, cv])

    att = _from_blocks(lax.map(block, (_to_blocks(q_rot), _to_blocks(q_raw)))).reshape(b, t, C_OUT_W)
    rwo, _ = _rwkv_mixer(rw, g_lo, s_ctx, rp)
    return jnp.concatenate([att, rwo.astype(att.dtype)], axis=-1) @ w_out


def _route(xf, wg, bg, we, be):
    gp = jax.nn.softmax((xf @ wg).astype(jnp.float32) + bg, axis=-1)
    p_g, g_idx = lax.top_k(gp, 1)
    el = ((xf @ we).astype(jnp.float32) + be).reshape(-1, N_GROUPS, EXPERTS_PER_GROUP)
    el = jnp.take_along_axis(el, g_idx[:, :, None], axis=1)[:, 0]
    p_e, e_loc = lax.top_k(jax.nn.softmax(el, axis=-1), TOP_K)
    weights = p_g * p_e / jnp.sum(p_e, axis=-1, keepdims=True)
    return g_idx * EXPERTS_PER_GROUP + e_loc, weights


def _grouped_experts(xf, idx, wts, w_gate, w_up, w_down):
    n_tok, d = xf.shape
    n_slots = n_tok * TOP_K
    flat_e = idx.reshape(-1)
    order = jnp.argsort(flat_e)
    sorted_e = flat_e[order]
    counts = jnp.bincount(flat_e, length=N_EXPERTS)
    padded = (counts + MOE_BLOCK - 1) // MOE_BLOCK * MOE_BLOCK
    pad_end = jnp.cumsum(padded)
    pad_start = pad_end - padded
    start = jnp.cumsum(counts) - counts
    dest_sorted = pad_start[sorted_e] + jnp.arange(n_slots) - start[sorted_e]
    n_blocks = -(-n_slots // MOE_BLOCK) + N_EXPERTS
    buf_tok = jnp.full((n_blocks * MOE_BLOCK,), n_tok, jnp.int32).at[dest_sorted].set(
        (order // TOP_K).astype(jnp.int32))
    block_e = jnp.minimum(jnp.searchsorted(pad_end, jnp.arange(n_blocks) * MOE_BLOCK, side='right'),
                          N_EXPERTS - 1)
    x_pad = jnp.concatenate([xf, jnp.zeros((1, d), xf.dtype)], axis=0)

    def run(args):
        tok, e = args
        xb = x_pad[tok]
        hb = jax.nn.silu(xb @ w_gate[e]) * (xb @ w_up[e])
        return hb @ w_down[e]

    out = lax.map(run, (buf_tok.reshape(n_blocks, MOE_BLOCK), block_e)).reshape(-1, d)
    dest = jnp.zeros((n_slots,), dest_sorted.dtype).at[order].set(dest_sorted)
    y = out[dest].reshape(n_tok, TOP_K, d)
    return jnp.einsum('tk,tkd->td', wts.astype(y.dtype), y)


def _moe(h, wg, bg, we, be, w_gate, w_up, w_down):
    b, t, d = h.shape
    xf = h.reshape(b * t, d)
    idx, wts = _route(xf, wg, bg, we, be)
    return _grouped_experts(xf, idx, wts, w_gate, w_up, w_down).reshape(b, t, d)


def setup_inputs(seed: int = 0) -> dict:
    key = jax.random.key(seed)
    ks = iter(jax.random.split(key, 64))
    f32 = jnp.float32

    def nrm(shape, scale):
        return jax.random.normal(next(ks), shape, f32) * scale

    def uni(shape, lo, hi):
        return jax.random.uniform(next(ks), shape, f32, lo, hi)

    D = D_MODEL
    return {
        'x_prompt': nrm((BATCH, SEQ, D), 1.0),
        'x_sample': nrm((DEC_BATCH, DEC_SEQ, D), 1.0),
        'c': nrm((DEC_BATCH, D), 1.0),
        'c_ctx': nrm((D,), 1.0),
        'cache_attn_k': nrm((DEC_BATCH, N_EVEN, PAST_LEN, A_KV_HEADS, HEAD_DIM), 1.0),
        'cache_attn_v': nrm((DEC_BATCH, N_EVEN, PAST_LEN, A_KV_HEADS, HEAD_DIM), 1.0),
        'cache_mla_ckv': nrm((DEC_BATCH, N_ODD, PAST_LEN, KV_LORA), 1.0),
        'cache_mla_kpe': nrm((DEC_BATCH, N_ODD, PAST_LEN, QK_ROPE), 1.0),
        'state_rwkv': nrm((DEC_BATCH, N_ODD, 2, D_HEADS, D_HEAD_SIZE, D_HEAD_SIZE), 0.3),
        'w_mod': nrm((DEPTH, D, 6 * D), 0.5 * D ** -0.5),
        'b_mod': nrm((DEPTH, 6 * D), 0.02),
        'norm_mix': 1.0 + nrm((DEPTH, D), 0.02),
        'norm_ffn': 1.0 + nrm((DEPTH, D), 0.02),
        'norm_final': 1.0 + nrm((D,), 0.02),
        'even_w_in': nrm((N_EVEN, D, EVEN_IN), D ** -0.5),
        'even_w_out': nrm((N_EVEN, EVEN_OUT, D), EVEN_OUT ** -0.5),
        'attn_sink': nrm((N_EVEN, A_HEADS), 0.5),
        'conv_w': nrm((N_EVEN, 3, B_WIDTH), 3 ** -0.5),
        'odd_w_in': nrm((N_ODD, D, ODD_IN), D ** -0.5),
        'odd_w_out': nrm((N_ODD, ODD_OUT, D), ODD_OUT ** -0.5),
        'mla_q_norm': 1.0 + nrm((N_ODD, Q_LORA), 0.02),
        'mla_kv_norm': 1.0 + nrm((N_ODD, KV_LORA), 0.02),
        'mla_w_uq': nrm((N_ODD, Q_LORA, C_HEADS * (QK_NOPE + QK_ROPE)), Q_LORA ** -0.5),
        'mla_w_ukv': nrm((N_ODD, KV_LORA, C_HEADS * (QK_NOPE + V_DIM)), KV_LORA ** -0.5),
        'rwkv_mu': uni((N_ODD, 2, SHIFT_W), 0.0, 1.0),
        'rwkv_w0': uni((N_ODD, 2, D_WIDTH), -4.0, 0.0),
        'rwkv_w_up': nrm((N_ODD, 2, DECAY_LORA, D_WIDTH), 0.5 * DECAY_LORA ** -0.5),
        'rwkv_a0': nrm((N_ODD, 2, D_WIDTH), 0.5),
        'rwkv_a_up': nrm((N_ODD, 2, AAA_LORA, D_WIDTH), 0.5 * AAA_LORA ** -0.5),
        'rwkv_k_k': 0.85 + nrm((N_ODD, 2, D_WIDTH), 0.05),
        'rwkv_k_a': 1.0 + nrm((N_ODD, 2, D_WIDTH), 0.05),
        'rwkv_r_k': nrm((N_ODD, 2, D_HEADS, D_HEAD_SIZE), 0.1),
        'rwkv_g_up': nrm((N_ODD, GATE_LORA, D_WIDTH), GATE_LORA ** -0.5),
        'rwkv_ln_w': 1.0 + nrm((N_ODD, D_WIDTH), 0.02),
        'rwkv_ln_b': nrm((N_ODD, D_WIDTH), 0.02),
        'router_group_w': nrm((DEPTH, D, N_GROUPS), D ** -0.5),
        'router_group_b': nrm((DEPTH, N_GROUPS), 0.01),
        'router_expert_w': nrm((DEPTH, D, N_EXPERTS), D ** -0.5),
        'router_expert_b': nrm((DEPTH, N_EXPERTS), 0.01),
        'expert_w_gate': nrm((DEPTH, N_EXPERTS, D, EXPERT_FF), D ** -0.5),
        'expert_w_up': nrm((DEPTH, N_EXPERTS, D, EXPERT_FF), D ** -0.5),
        'expert_w_down': nrm((DEPTH, N_EXPERTS, EXPERT_FF, D), EXPERT_FF ** -0.5),
    }


def reference(x_prompt, x_sample, c, c_ctx, cache_attn_k, cache_attn_v, cache_mla_ckv, cache_mla_kpe,
              state_rwkv, w_mod, b_mod, norm_mix, norm_ffn, norm_final, even_w_in, even_w_out, attn_sink,
              conv_w, odd_w_in, odd_w_out, mla_q_norm, mla_kv_norm, mla_w_uq, mla_w_ukv, rwkv_mu, rwkv_w0,
              rwkv_w_up, rwkv_a0, rwkv_a_up, rwkv_k_k, rwkv_k_a, rwkv_r_k, rwkv_g_up, rwkv_ln_w, rwkv_ln_b,
              router_group_w, router_group_b, router_expert_w, router_expert_b, expert_w_gate, expert_w_up,
              expert_w_down):
    x_c, x_l = x_prompt, x_sample
    new_k, new_v, new_ckv, new_kpe, new_s = [], [], [], [], []
    for l in range(DEPTH):
        m_c = _modulation(c_ctx, w_mod[l], b_mod[l])
        m_l = [m[:, None, :] for m in _modulation(c, w_mod[l], b_mod[l])]
        h_c = _modulate(x_c, norm_mix[l], m_c[0], m_c[1])
        h_l = _modulate(x_l, norm_mix[l], m_l[0], m_l[1])
        i = l // 2
        if l % 2 == 0:
            y_c, k_c, v_c = _even_mixer_context(h_c, even_w_in[i], even_w_out[i], attn_sink[i], conv_w[i])
            y_l = _even_mixer_latent(h_l, cache_attn_k[:, i], cache_attn_v[:, i], even_w_in[i], even_w_out[i],
                                     attn_sink[i], conv_w[i])
            new_k.append(k_c)
            new_v.append(v_c)
        else:
            mp = dict(q_norm=mla_q_norm[i], kv_norm=mla_kv_norm[i], w_uq=mla_w_uq[i], w_ukv=mla_w_ukv[i])
            rp = dict(mu=rwkv_mu[i], w0=rwkv_w0[i], w_up=rwkv_w_up[i], a0=rwkv_a0[i], a_up=rwkv_a_up[i],
                      k_k=rwkv_k_k[i], k_a=rwkv_k_a[i], r_k=rwkv_r_k[i], g_up=rwkv_g_up[i],
                      ln_w=rwkv_ln_w[i], ln_b=rwkv_ln_b[i])
            y_c, ckv_c, kpe_c, s_c = _odd_mixer_context(h_c, odd_w_in[i], odd_w_out[i], mp, rp)
            y_l = _odd_mixer_latent(h_l, cache_mla_ckv[:, i], cache_mla_kpe[:, i], state_rwkv[:, i],
                                    odd_w_in[i], odd_w_out[i], mp, rp)
            new_ckv.append(ckv_c)
            new_kpe.append(kpe_c)
            new_s.append(s_c)
        x_c = x_c + m_c[2] * y_c
        x_l = x_l + m_l[2] * y_l
        moe_args = (router_group_w[l], router_group_b[l], router_expert_w[l], router_expert_b[l],
                    expert_w_gate[l], expert_w_up[l], expert_w_down[l])
        x_c = x_c + m_c[5] * _moe(_modulate(x_c, norm_ffn[l], m_c[3], m_c[4]), *moe_args)
        x_l = x_l + m_l[5] * _moe(_modulate(x_l, norm_ffn[l], m_l[3], m_l[4]), *moe_args)
    y_prompt = _rmsnorm(x_c, norm_final)
    y_sample = _rmsnorm(x_l, norm_final)
    return (y_prompt, y_sample, jnp.stack(new_k, axis=1), jnp.stack(new_v, axis=1),
            jnp.stack(new_ckv, axis=1), jnp.stack(new_kpe, axis=1), jnp.stack(new_s, axis=1))
```

```python
import functools

import jax
import jax.numpy as jnp
from jax import lax
from jax.experimental import pallas as pl
from jax.experimental.pallas import tpu as pltpu

F32 = jnp.float32
BF16 = jnp.bfloat16
HIGHEST = lax.Precision.HIGHEST

D_MODEL = 4096
BATCH = 16
SEQ = 256
DEPTH = 4
DEC_BATCH = 2
DEC_SEQ = 1024
PAST_LEN = 512
GRID_W = 64
WINDOW = 128
ROPE_BASE = 10000.0
NORM_EPS = 1e-6
NEG_INF = -1e30

HEAD_DIM = 128
A_HEADS = 16
A_KV_HEADS = 4
A_GROUP = A_HEADS // A_KV_HEADS
A_Q_W = A_HEADS * HEAD_DIM
A_KV_W = A_KV_HEADS * HEAD_DIM
B_WIDTH = D_MODEL // 2
C_HEADS = 16
Q_LORA = D_MODEL // 4
KV_LORA = D_MODEL // 8
QK_NOPE = 128
QK_ROPE = 64
V_DIM = 128
C_OUT_W = C_HEADS * V_DIM
D_WIDTH = D_MODEL // 2
D_HEAD_SIZE = 64
D_HEADS = D_WIDTH // D_HEAD_SIZE
DECAY_LORA = 64
AAA_LORA = 64
GATE_LORA = 256
GN_EPS = 64e-5
SHIFT_W = 3 * D_WIDTH + DECAY_LORA + AAA_LORA
N_GROUPS = 8
EXPERTS_PER_GROUP = 8
N_EXPERTS = N_GROUPS * EXPERTS_PER_GROUP
TOP_K = 2
EXPERT_FF = 512

N_CTX_TOK = BATCH * SEQ
N_LAT_TOK = DEC_BATCH * DEC_SEQ
N_TOK = N_CTX_TOK + N_LAT_TOK
N_MOD_GROUPS = 1 + DEC_BATCH

LANES = 128
VMEM_LIMIT_BYTES = 56 * 1024 * 1024
MOE_ROWS = 256
MOE_FF_TILE = 256
RWKV_CHUNK = 32
RWKV_PAIR = 2 * D_HEAD_SIZE
N_PAIRS = D_WIDTH // RWKV_PAIR


def _cparams(*sem):
    return pltpu.CompilerParams(dimension_semantics=sem, vmem_limit_bytes=VMEM_LIMIT_BYTES)


def _row_group(i, tm):
    nc = N_CTX_TOK // tm
    nl = DEC_SEQ // tm
    return jnp.where(i < nc, 0, 1 + (i - nc) // nl)


def _normmod_kernel(x_ref, g_ref, shift_ref, scale_ref, *out_refs):
    x = x_ref[...]
    y = x * lax.rsqrt(jnp.mean(x * x, axis=-1, keepdims=True) + NORM_EPS)
    y = y * g_ref[...]
    y = y * (1 + scale_ref[0]) + shift_ref[0]
    for o in out_refs:
        o[...] = y.astype(o.dtype)


def _normmod(x, g, shift, scale, out_dtypes, tm=256):
    n, d = x.shape
    grp = lambda i: (_row_group(i, tm), 0, 0)
    return pl.pallas_call(
        _normmod_kernel,
        grid=(n // tm,),
        in_specs=[pl.BlockSpec((tm, d), lambda i: (i, 0)),
                  pl.BlockSpec((1, d), lambda i: (0, 0)),
                  pl.BlockSpec((1, 1, d), grp),
                  pl.BlockSpec((1, 1, d), grp)],
        out_specs=[pl.BlockSpec((tm, d), lambda i: (i, 0)) for _ in out_dtypes],
        out_shape=[jax.ShapeDtypeStruct((n, d), dt) for dt in out_dtypes],
        compiler_params=_cparams("parallel"),
        name="normmod",
    )(x, g.reshape(1, d), shift.reshape(-1, 1, d), scale.reshape(-1, 1, d))


def _mm_kernel(*refs, nk, has_res, precision):
    if has_res:
        a_ref, w_ref, res_ref, gate_ref, o_ref, acc_ref = refs
    else:
        a_ref, w_ref, o_ref, acc_ref = refs
    k = pl.program_id(2)
    a = a_ref[...]
    w = w_ref[...]
    if precision is None:
        a = a.astype(BF16)
        w = w.astype(BF16)
    part = jnp.dot(a, w, preferred_element_type=F32, precision=precision)

    def finish(acc):
        if has_res:
            acc = res_ref[...] + gate_ref[0] * acc
        o_ref[...] = acc.astype(o_ref.dtype)

    if nk == 1:
        finish(part)
    else:
        @pl.when(k == 0)
        def _():
            acc_ref[...] = part

        @pl.when(k > 0)
        def _():
            acc_ref[...] += part

        @pl.when(k == nk - 1)
        def _():
            finish(acc_ref[...])


def _mm(a, w, w_idx=(), *, out_dtype=F32, res=None, gate=None, precision=None,
        tm=1024, tn=1024, tk=1024, name="mm"):
    m, kdim = a.shape
    n = w.shape[-1]
    assert w.shape[-2] == kdim
    tm, tn, tk = min(tm, m), min(tn, n), min(tk, kdim)
    assert m % tm == 0 and kdim % tk == 0
    nk = kdim // tk
    lead = tuple(w_idx)
    in_specs = [pl.BlockSpec((tm, tk), lambda i, j, k: (i, k)),
                pl.BlockSpec((None,) * len(lead) + (tk, tn), lambda i, j, k: lead + (k, j))]
    args = [a, w]
    has_res = res is not None
    if has_res:
        in_specs += [pl.BlockSpec((tm, tn), lambda i, j, k: (i, j)),
                     pl.BlockSpec((1, 1, tn), lambda i, j, k: (_row_group(i, tm), 0, j))]
        args += [res, gate.reshape(N_MOD_GROUPS, 1, n)]
    return pl.pallas_call(
        functools.partial(_mm_kernel, nk=nk, has_res=has_res, precision=precision),
        grid=(m // tm, pl.cdiv(n, tn), nk),
        in_specs=in_specs,
        out_specs=pl.BlockSpec((tm, tn), lambda i, j, k: (i, j)),
        out_shape=jax.ShapeDtypeStruct((m, n), out_dtype),
        scratch_shapes=[pltpu.VMEM((tm, tn), F32)],
        compiler_params=_cparams("parallel", "parallel", "arbitrary"),
        name=name,
    )(*args)


def _dot_nt(a, b, precision=None):
    return lax.dot_general(a, b, (((1,), (1,)), ((), ())), preferred_element_type=F32,
                           precision=precision)


def _softmax_read(scores, values, sink_col):
    m = scores[0].max(axis=-1, keepdims=True)
    for s in scores[1:]:
        m = jnp.maximum(m, s.max(axis=-1, keepdims=True))
    if sink_col is not None:
        m = jnp.maximum(m, sink_col)
    den = jnp.exp(sink_col - m) if sink_col is not None else 0.0
    acc = None
    for s, v in zip(scores, values):
        p = jnp.exp(s - m)
        den = den + p.sum(axis=-1, keepdims=True)
        pv = jnp.dot(p.astype(BF16), v, preferred_element_type=F32)
        acc = pv if acc is None else acc + pv
    return acc / den


def _gqa_kernel(*refs, tq, has2, band):
    if has2:
        sink_ref, q1_ref, k1_ref, v1_ref, q2_ref, k2_ref, v2_ref, o_ref = refs
    else:
        sink_ref, q1_ref, k1_ref, v1_ref, o_ref = refs
    h = pl.program_id(1)
    qi = pl.program_id(2)
    scale = HEAD_DIM ** -0.5

    def stack(q_ref):
        q = q_ref[...]
        return jnp.concatenate([q[:, g * HEAD_DIM:(g + 1) * HEAD_DIM] for g in range(A_GROUP)],
                               axis=0).astype(BF16)

    s1 = _dot_nt(stack(q1_ref), k1_ref[...].astype(BF16)) * scale
    if band:
        qpos = qi * tq + lax.broadcasted_iota(jnp.int32, s1.shape, 0) % tq
        kpos = lax.broadcasted_iota(jnp.int32, s1.shape, 1)
        s1 = jnp.where(jnp.abs(kpos - qpos) <= WINDOW, s1, NEG_INF)
    scores, values = [s1], [v1_ref[...].astype(BF16)]
    if has2:
        scores.append(_dot_nt(stack(q2_ref), k2_ref[...].astype(BF16)) * scale)
        values.append(v2_ref[...].astype(BF16))
    sink_col = jnp.concatenate([jnp.full((tq, 1), sink_ref[h * A_GROUP + g], F32)
                                for g in range(A_GROUP)], axis=0)
    out = _softmax_read(scores, values, sink_col)
    for g in range(A_GROUP):
        o_ref[:, g * HEAD_DIM:(g + 1) * HEAD_DIM] = out[g * tq:(g + 1) * tq]


def _tok_spec(rows, width, row0, rows_per_batch, col0, per_q):
    assert row0 % rows == 0 and rows_per_batch % rows == 0 and col0 % width == 0
    r0, rb, c0 = row0 // rows, rows_per_batch // rows, col0 // width
    if per_q:
        return pl.BlockSpec((rows, width), lambda b, h, qi: (r0 + b * rb + qi, c0 + h))
    return pl.BlockSpec((rows, width), lambda b, h, qi: (r0 + b * rb, c0 + h))


def _gqa_attn(sink, q1, k1, v1, seg2=None, *, nb, tq_len, band, out_rows, tq=128):
    gw = A_GROUP * HEAD_DIM
    qspec = lambda t: _tok_spec(tq, gw, t[1], tq_len, t[2], True)
    kspec = lambda t, tk: _tok_spec(tk, HEAD_DIM, t[1], tk, t[2], False)
    in_specs = [pl.BlockSpec(memory_space=pltpu.SMEM), qspec(q1), kspec(k1, tq_len), kspec(v1, tq_len)]
    args = [sink, q1[0], k1[0], v1[0]]
    if seg2 is not None:
        q2, k2, v2, tk2 = seg2
        in_specs += [qspec(q2), kspec(k2, tk2), kspec(v2, tk2)]
        args += [q2[0], k2[0], v2[0]]
    return pl.pallas_call(
        functools.partial(_gqa_kernel, tq=tq, has2=seg2 is not None, band=band),
        grid=(nb, A_KV_HEADS, tq_len // tq),
        in_specs=in_specs,
        out_specs=_tok_spec(tq, gw, 0, tq_len, 0, True),
        out_shape=jax.ShapeDtypeStruct((out_rows, A_Q_W), F32),
        compiler_params=_cparams("parallel", "parallel", "arbitrary"),
        name="gqa_attn",
    )(*args)


def _mla_kernel(*refs, has2):
    if has2:
        qn_ref, qp1_ref, kn1_ref, kp1_ref, v1_ref, qp2_ref, kn2_ref, kp2_ref, v2_ref, o_ref = refs
    else:
        qn_ref, qp1_ref, kn1_ref, kp1_ref, v1_ref, o_ref = refs
    scale = (QK_NOPE + QK_ROPE) ** -0.5
    lane = lax.broadcasted_iota(jnp.int32, qp1_ref.shape, 1)
    for hh in range(2):
        cols = slice(hh * QK_NOPE, (hh + 1) * QK_NOPE)
        mine = (lane < QK_ROPE) if hh == 0 else (lane >= QK_ROPE)
        qn = qn_ref[:, cols]

        def score(qp_ref, kn_ref, kp_ref):
            q = jnp.concatenate([qn, jnp.where(mine, qp_ref[...], 0.0)], axis=1).astype(BF16)
            k = jnp.concatenate([kn_ref[:, cols], kp_ref[...]], axis=1).astype(BF16)
            return _dot_nt(q, k) * scale

        scores = [score(qp1_ref, kn1_ref, kp1_ref)]
        values = [v1_ref[:, cols].astype(BF16)]
        if has2:
            scores.append(score(qp2_ref, kn2_ref, kp2_ref))
            values.append(v2_ref[:, cols].astype(BF16))
        o_ref[:, cols] = _softmax_read(scores, values, None)


def _mla_attn(qn, qp1, kn1, kp1, v1, seg2=None, *, nb, tq_len, out_rows, tq=256):
    pw = 2 * QK_NOPE
    qspec = lambda t, w: _tok_spec(tq, w, t[1], tq_len, t[2], True)
    kspec = lambda t, w, tk: _tok_spec(tk, w, t[1], tk, t[2], False)
    kpspec = lambda t, tk: pl.BlockSpec((tk, LANES), lambda b, h, qi: (t[1] // tk + b, 0))
    in_specs = [qspec(qn, pw), qspec(qp1, LANES), kspec(kn1, pw, tq_len), kpspec(kp1, tq_len),
                kspec(v1, pw, tq_len)]
    args = [qn[0], qp1[0], kn1[0], kp1[0], v1[0]]
    if seg2 is not None:
        qp2, kn2, kp2, v2, tk2 = seg2
        in_specs += [qspec(qp2, LANES), kspec(kn2, pw, tk2), kpspec(kp2, tk2), kspec(v2, pw, tk2)]
        args += [qp2[0], kn2[0], kp2[0], v2[0]]
    return pl.pallas_call(
        functools.partial(_mla_kernel, has2=seg2 is not None),
        grid=(nb, C_HEADS // 2, tq_len // tq),
        in_specs=in_specs,
        out_specs=_tok_spec(tq, pw, 0, tq_len, 0, True),
        out_shape=jax.ShapeDtypeStruct((out_rows, C_OUT_W), F32),
        compiler_params=_cparams("parallel", "parallel", "arbitrary"),
        name="mla_attn",
    )(*args)


def _rwkv_kernel(r_ref, lw_ref, k_ref, v_ref, a_ref, b_ref, s0_ref, y_ref, s_ref, *, seq, pairs, reverse):
    c = RWKV_CHUNK
    c2 = 2 * c
    nchunk = seq // c
    row = lax.broadcasted_iota(jnp.int32, (c, c), 0)
    col = lax.broadcasted_iota(jnp.int32, (c, c), 1)
    tri = ((col >= row) if reverse else (col <= row)).astype(F32)
    last = 0 if reverse else c - 1
    lane = lax.broadcasted_iota(jnp.int32, (c, RWKV_PAIR), 1)
    head0 = lane < D_HEAD_SIZE
    grow = lax.broadcasted_iota(jnp.int32, (c2, 2 * c2), 0)
    gcol = lax.broadcasted_iota(jnp.int32, (c2, 2 * c2), 1)
    gcol_in = jnp.where(gcol >= c2, gcol - c2, gcol)
    before = (gcol_in > grow) if reverse else (gcol_in < grow)
    upto = (gcol_in >= grow) if reverse else (gcol_in <= grow)
    eye = (lax.broadcasted_iota(jnp.int32, (c2, c2), 0) ==
           lax.broadcasted_iota(jnp.int32, (c2, c2), 1)).astype(F32)
    n_double = c.bit_length() - 2
    dot = functools.partial(jnp.dot, preferred_element_type=F32, precision=HIGHEST)

    def stacked(x):
        return jnp.concatenate([jnp.where(head0, x, 0.0), jnp.where(head0, 0.0, x)], axis=0)

    s_ref[0] = s0_ref[0]

    def chunk(ci, carry):
        cidx = (nchunk - 1 - ci) if reverse else ci
        rows = pl.ds(pl.multiple_of(cidx * c, c), c)
        for p in range(pairs):
            lanes = slice(p * RWKV_PAIR, (p + 1) * RWKV_PAIR)
            r, lw, k = r_ref[0, rows, lanes], lw_ref[0, rows, lanes], k_ref[0, rows, lanes]
            v, a, b = v_ref[0, rows, lanes], a_ref[0, rows, lanes], b_ref[0, rows, lanes]
            s = s_ref[0, p]
            lg = dot(tri, lw)
            lg_end = lg[last:last + 1]
            g_inv = jnp.exp(-lg)
            g_end = jnp.exp(lg_end - lg)
            at, rt = stacked(a * jnp.exp(lg - lw)), stacked(r * jnp.exp(lg))
            bt, kt = stacked(b * g_inv), stacked(k * g_inv)
            vs = stacked(v)
            gram = _dot_nt(jnp.concatenate([at, rt], axis=0), jnp.concatenate([bt, kt], axis=0),
                           precision=HIGHEST)
            g_u = jnp.where(before, gram[:c2], 0.0)
            g_y = jnp.where(upto, gram[c2:], 0.0)
            low = g_u[:, :c2]
            tinv = eye + low
            pw = low
            for _ in range(n_double):
                pw = dot(pw, pw)
                tinv = tinv + dot(tinv, pw)
            x = _dot_nt(at, s, precision=HIGHEST) + dot(jnp.where(gcol >= c2, g_u, 0.0),
                                                         jnp.concatenate([jnp.zeros_like(vs), vs], axis=0))
            us = dot(tinv, x)
            uv = jnp.concatenate([us, vs], axis=0)
            ys = _dot_nt(rt, s, precision=HIGHEST) + dot(g_y, uv)
            y_ref[0, rows, lanes] = ys[:c] + ys[c:]
            bk = jnp.concatenate([stacked(b * g_end), stacked(k * g_end)], axis=0)
            s_ref[0, p] = s * jnp.exp(lg_end) + lax.dot_general(
                uv, bk, (((0,), (0,)), ((), ())), preferred_element_type=F32, precision=HIGHEST)
        return carry

    lax.fori_loop(0, nchunk, chunk, 0)


def _rwkv_scan(r, lw, k, v, a, b, s0, *, reverse, pairs):
    nb, seq, _ = r.shape
    assert 4 * RWKV_CHUNK == RWKV_PAIR and seq % RWKV_CHUNK == 0 and N_PAIRS % pairs == 0
    tok = pl.BlockSpec((1, seq, pairs * RWKV_PAIR), lambda bi, pi: (bi, 0, pi))
    st = pl.BlockSpec((1, pairs, RWKV_PAIR, RWKV_PAIR), lambda bi, pi: (bi, pi, 0, 0))
    return pl.pallas_call(
        functools.partial(_rwkv_kernel, seq=seq, pairs=pairs, reverse=reverse),
        grid=(nb, N_PAIRS // pairs),
        in_specs=[tok] * 6 + [st],
        out_specs=[tok, st],
        out_shape=[jax.ShapeDtypeStruct(r.shape, F32), jax.ShapeDtypeStruct(s0.shape, F32)],
        compiler_params=_cparams("parallel", "parallel"),
        name="rwkv_rev" if reverse else "rwkv_fwd",
    )(r, lw, k, v, a, b, s0)


def _moe_kernel(be_ref, nv_ref, x_ref, wg_ref, wu_ref, wd_ref, o_ref):
    blk = pl.program_id(0)
    f = pl.program_id(1)
    used = blk < nv_ref[0]

    @pl.when(used)
    def _():
        x = x_ref[...]
        g = jnp.dot(x, wg_ref[...].astype(BF16), preferred_element_type=F32)
        u = jnp.dot(x, wu_ref[...].astype(BF16), preferred_element_type=F32)
        h = (g * jax.nn.sigmoid(g)) * u
        y = jnp.dot(h.astype(BF16), wd_ref[...].astype(BF16), preferred_element_type=F32)

        @pl.when(f == 0)
        def _():
            o_ref[...] = y

        @pl.when(f > 0)
        def _():
            o_ref[...] += y

    @pl.when(jnp.logical_and(jnp.logical_not(used), f == 0))
    def _():
        o_ref[...] = jnp.zeros_like(o_ref)


def _moe_experts(layer, block_e, n_used, xg, w_gate, w_up, w_down):
    n_blocks = block_e.shape[0]
    d = xg.shape[1]
    wspec = pl.BlockSpec((None, None, d, MOE_FF_TILE), lambda i, f, be, nv: (layer, be[i], 0, f))
    return pl.pallas_call(
        _moe_kernel,
        grid_spec=pltpu.PrefetchScalarGridSpec(
            num_scalar_prefetch=2,
            grid=(n_blocks, EXPERT_FF // MOE_FF_TILE),
            in_specs=[pl.BlockSpec((MOE_ROWS, d), lambda i, f, be, nv: (i, 0)),
                      wspec, wspec,
                      pl.BlockSpec((None, None, MOE_FF_TILE, d), lambda i, f, be, nv: (layer, be[i], f, 0))],
            out_specs=pl.BlockSpec((MOE_ROWS, d), lambda i, f, be, nv: (i, 0))),
        out_shape=jax.ShapeDtypeStruct((n_blocks * MOE_ROWS, d), F32),
        compiler_params=_cparams("arbitrary", "arbitrary"),
        name="moe_experts",
    )(block_e, n_used, xg, w_gate, w_up, w_down)


def _moe(layer, h_f32, h_bf16, w_router, b_router, w_gate, w_up, w_down):
    n_tok = h_f32.shape[0]
    logits = _mm(h_f32, w_router, precision=HIGHEST, tn=LANES, name="router")
    logits = logits[:, :N_GROUPS + N_EXPERTS] + b_router
    gp = jax.nn.softmax(logits[:, :N_GROUPS], axis=-1)
    p_g, g_idx = lax.top_k(gp, 1)
    el = logits[:, N_GROUPS:].reshape(-1, N_GROUPS, EXPERTS_PER_GROUP)
    el = jnp.take_along_axis(el, g_idx[:, :, None], axis=1)[:, 0]
    p_e, e_loc = lax.top_k(jax.nn.softmax(el, axis=-1), TOP_K)
    wts = p_g * p_e / jnp.sum(p_e, axis=-1, keepdims=True)
    idx = g_idx * EXPERTS_PER_GROUP + e_loc

    n_slots = n_tok * TOP_K
    flat_e = idx.reshape(-1).astype(jnp.int32)
    order = jnp.argsort(flat_e)
    sorted_e = flat_e[order]
    counts = jnp.bincount(flat_e, length=N_EXPERTS)
    padded = (counts + MOE_ROWS - 1) // MOE_ROWS * MOE_ROWS
    pad_end = jnp.cumsum(padded)
    pad_start = pad_end - padded
    start = jnp.cumsum(counts) - counts
    dest_sorted = pad_start[sorted_e] + jnp.arange(n_slots) - start[sorted_e]
    n_blocks = n_slots // MOE_ROWS + N_EXPERTS
    buf_tok = jnp.full((n_blocks * MOE_ROWS,), n_tok, jnp.int32).at[dest_sorted].set(
        (order // TOP_K).astype(jnp.int32))
    block_e = jnp.minimum(jnp.searchsorted(pad_end, jnp.arange(n_blocks) * MOE_ROWS, side='right'),
                          N_EXPERTS - 1).astype(jnp.int32)
    n_used = (pad_end[-1] // MOE_ROWS).astype(jnp.int32).reshape(1)
    x_pad = jnp.concatenate([h_bf16, jnp.zeros((1, h_bf16.shape[1]), BF16)], axis=0)
    out = _moe_experts(layer, block_e, n_used, x_pad[buf_tok], w_gate, w_up, w_down)
    dest = jnp.zeros((n_slots,), dest_sorted.dtype).at[order].set(dest_sorted)
    y = out[dest].reshape(n_tok, TOP_K, -1)
    return jnp.einsum('tk,tkd->td', wts, y)


def _axial_rope_tables(n_tok, rot_dim):
    t = jnp.arange(n_tok)
    rows = (t // GRID_W).astype(F32)
    cols = (t % GRID_W).astype(F32)
    axis_dim = rot_dim // 2
    inv = ROPE_BASE ** (-jnp.arange(0, axis_dim, 2, dtype=F32) / axis_dim)
    ang_r = rows[:, None] * inv[None, :]
    ang_c = cols[:, None] * inv[None, :]
    return jnp.cos(ang_r), jnp.sin(ang_r), jnp.cos(ang_c), jnp.sin(ang_c)


def _rope_rotate(x, cos, sin):
    half = x.shape[-1] // 2
    c, s = cos[None, :, None, :], sin[None, :, None, :]
    x1, x2 = x[..., :half], x[..., half:]
    return jnp.concatenate([x1 * c - x2 * s, x2 * c + x1 * s], axis=-1)


def _apply_axial_rope(x, tables):
    cos_r, sin_r, cos_c, sin_c = tables
    h = x.shape[-1] // 2
    return jnp.concatenate([_rope_rotate(x[..., :h], cos_r, sin_r),
                            _rope_rotate(x[..., h:], cos_c, sin_c)], axis=-1)


def _short_conv(bg, cg, u, conv_w, nb):
    z = (cg * u).reshape(nb, -1, B_WIDTH)
    zp = jnp.pad(z, ((0, 0), (1, 1), (0, 0)))
    conv = conv_w[0] * zp[:, :-2] + conv_w[1] * zp[:, 1:-1] + conv_w[2] * zp[:, 2:]
    return bg * conv.reshape(-1, B_WIDTH)


def _even_mixer(h, i, cache_k, cache_v, w_in, w_out, sink, conv_w, x, gate):
    p = _mm(h, w_in, (i,), name="even_in")
    o_k, o_v, o_b = A_Q_W, A_Q_W + A_KV_W, A_Q_W + 2 * A_KV_W
    att_c = _gqa_attn(sink[i], (p, 0, 0), (p, 0, o_k), (p, 0, o_v),
                      nb=BATCH, tq_len=SEQ, band=False, out_rows=N_CTX_TOK)
    pl_ = p[N_CTX_TOK:]
    tabs = _axial_rope_tables(DEC_SEQ, HEAD_DIM)
    q_rot = _apply_axial_rope(pl_[:, :A_Q_W].reshape(DEC_BATCH, DEC_SEQ, A_HEADS, HEAD_DIM), tabs)
    k_rot = _apply_axial_rope(pl_[:, o_k:o_v].reshape(DEC_BATCH, DEC_SEQ, A_KV_HEADS, HEAD_DIM), tabs)
    ck = cache_k[:, i].reshape(DEC_BATCH * PAST_LEN, A_KV_W)
    cv = cache_v[:, i].reshape(DEC_BATCH * PAST_LEN, A_KV_W)
    att_l = _gqa_attn(sink[i], (q_rot.reshape(N_LAT_TOK, A_Q_W), 0, 0),
                      (k_rot.reshape(N_LAT_TOK, A_KV_W), 0, 0), (p, N_CTX_TOK, o_v),
                      ((p, N_CTX_TOK, 0), (ck, 0, 0), (cv, 0, 0), PAST_LEN),
                      nb=DEC_BATCH, tq_len=DEC_SEQ, band=True, out_rows=N_LAT_TOK)
    bg, cg, u = (p[:, o_b + j * B_WIDTH:o_b + (j + 1) * B_WIDTH] for j in range(3))
    conv = jnp.concatenate([_short_conv(bg[:N_CTX_TOK], cg[:N_CTX_TOK], u[:N_CTX_TOK], conv_w[i], BATCH),
                            _short_conv(bg[N_CTX_TOK:], cg[N_CTX_TOK:], u[N_CTX_TOK:], conv_w[i], DEC_BATCH)],
                           axis=0)
    cat = jnp.concatenate([jnp.concatenate([att_c, att_l], axis=0), conv], axis=-1).astype(BF16)
    x = _mm(cat, w_out, (i,), res=x, gate=gate, name="even_out")
    new_k = p[:N_CTX_TOK, o_k:o_v].reshape(BATCH, SEQ, A_KV_HEADS, HEAD_DIM)
    new_v = p[:N_CTX_TOK, o_v:o_b].reshape(BATCH, SEQ, A_KV_HEADS, HEAD_DIM)
    return x, new_k, new_v


def _rms(x, g):
    return x * lax.rsqrt(jnp.mean(x * x, axis=-1, keepdims=True) + NORM_EPS) * g


def _rwkv_group(rw, g_lo, s0, i, rp, nb, pairs):
    seq = rw.shape[0] // nb
    rw3 = rw.reshape(nb, seq, SHIFT_W)
    outs, finals = [], []
    for d in range(2):
        if d == 0:
            nbr = jnp.pad(rw3, ((0, 0), (1, 0), (0, 0)))[:, :-1]
        else:
            nbr = jnp.pad(rw3, ((0, 0), (0, 1), (0, 0)))[:, 1:]
        z = (rw3 + rp['mu'][i, d] * (nbr - rw3)).reshape(nb * seq, SHIFT_W)
        r, k, v = (z[:, j * D_WIDTH:(j + 1) * D_WIDTH] for j in range(3))
        w_lo = z[:, 3 * D_WIDTH:3 * D_WIDTH + DECAY_LORA]
        a_lo = z[:, 3 * D_WIDTH + DECAY_LORA:]
        w_raw = rp['w0'][i, d] + _mm(jnp.tanh(w_lo), rp['w_up'], (i, d), name="rwkv_decay")
        lw = -jnp.exp(-jax.nn.softplus(-w_raw) - 0.5)
        a = jax.nn.sigmoid(rp['a0'][i, d] + _mm(a_lo, rp['a_up'], (i, d), name="rwkv_aaa"))
        heads = lambda t: t.reshape(nb * seq, D_HEADS, D_HEAD_SIZE)
        kk = heads(k * rp['k_k'][i, d])
        kk = kk / jnp.maximum(jnp.sqrt(jnp.sum(kk * kk, axis=-1, keepdims=True)), 1e-12)
        kk = kk.reshape(nb * seq, D_WIDTH)
        k = k * (1 + (a - 1) * rp['k_a'][i, d])
        to3 = lambda t: t.reshape(nb, seq, D_WIDTH)
        sd = s0[:, d].reshape(nb, N_PAIRS, 2, D_HEAD_SIZE, D_HEAD_SIZE)
        eye2 = jnp.eye(2, dtype=F32)
        s_bd = jnp.einsum('bphij,hg->bphigj', sd, eye2).reshape(nb, N_PAIRS, RWKV_PAIR, RWKV_PAIR)
        y, s_fin = _rwkv_scan(to3(r), to3(lw), to3(k), to3(v), to3(-kk), to3(kk * a), s_bd,
                              reverse=(d == 1), pairs=pairs)
        s_fin = s_fin.reshape(nb, N_PAIRS, 2, D_HEAD_SIZE, 2, D_HEAD_SIZE)
        s_fin = jnp.stack([s_fin[:, :, 0, :, 0], s_fin[:, :, 1, :, 1]], axis=2)
        finals.append(s_fin.reshape(nb, D_HEADS, D_HEAD_SIZE, D_HEAD_SIZE))
        yh = heads(y.reshape(nb * seq, D_WIDTH))
        mu = jnp.mean(yh, axis=-1, keepdims=True)
        var = jnp.mean(jnp.square(yh - mu), axis=-1, keepdims=True)
        yn = ((yh - mu) * lax.rsqrt(var + GN_EPS)).reshape(nb * seq, D_WIDTH) * rp['ln_w'][i] + rp['ln_b'][i]
        bonus = jnp.sum(heads(r) * heads(k) * rp['r_k'][i, d], axis=-1, keepdims=True) * heads(v)
        outs.append(yn + bonus.reshape(nb * seq, D_WIDTH))
    g = _mm(jax.nn.sigmoid(g_lo), rp['g_up'], (i,), name="rwkv_gate")
    return (outs[0] + outs[1]) * g, jnp.stack(finals, axis=1)


def _odd_mixer(h, i, cache_ckv, cache_kpe, state, w_in, w_out, mp, rp, x, gate):
    p = _mm(h, w_in, (i,), name="odd_in")
    o1 = Q_LORA
    o2 = o1 + KV_LORA
    o3 = o2 + QK_ROPE
    o4 = o3 + SHIFT_W
    q_down, kv_down, k_pe, rw, g_lo = p[:, :o1], p[:, o1:o2], p[:, o2:o3], p[:, o3:o4], p[:, o4:]
    w_uq = mp['w_uq'][i].reshape(Q_LORA, C_HEADS, QK_NOPE + QK_ROPE)
    w_uq = jnp.concatenate([w_uq[:, :, :QK_NOPE].reshape(Q_LORA, -1), w_uq[:, :, QK_NOPE:].reshape(Q_LORA, -1)],
                           axis=1)
    w_ukv = mp['w_ukv'][i].reshape(KV_LORA, C_HEADS, QK_NOPE + V_DIM)
    w_ukv = jnp.concatenate([w_ukv[:, :, :QK_NOPE].reshape(KV_LORA, -1), w_ukv[:, :, QK_NOPE:].reshape(KV_LORA, -1)],
                            axis=1)
    q = _mm(_rms(q_down, mp['q_norm'][i]), w_uq, name="mla_uq")
    c_kv = _rms(kv_down, mp['kv_norm'][i])
    kv = _mm(c_kv, w_ukv, name="mla_ukv")
    kv_ctx = _mm(cache_ckv[:, i].reshape(DEC_BATCH * PAST_LEN, KV_LORA), w_ukv, name="mla_ukv_cache")
    n_nope = C_HEADS * QK_NOPE
    dup = lambda t: jnp.concatenate([t, t], axis=-1)
    kp = dup(k_pe)
    att_c = _mla_attn((q, 0, 0), (q, 0, n_nope), (kv, 0, 0), (kp, 0, 0), (kv, 0, n_nope),
                      nb=BATCH, tq_len=SEQ, out_rows=N_CTX_TOK)
    tabs = _axial_rope_tables(DEC_SEQ, QK_ROPE)
    qpe_rot = _apply_axial_rope(q[N_CTX_TOK:, n_nope:].reshape(DEC_BATCH, DEC_SEQ, C_HEADS, QK_ROPE), tabs)
    kpe_rot = _apply_axial_rope(k_pe[N_CTX_TOK:].reshape(DEC_BATCH, DEC_SEQ, 1, QK_ROPE), tabs)
    kp_ctx = dup(cache_kpe[:, i].reshape(DEC_BATCH * PAST_LEN, QK_ROPE))
    att_l = _mla_attn((q, N_CTX_TOK, 0), (qpe_rot.reshape(N_LAT_TOK, -1), 0, 0), (kv, N_CTX_TOK, 0),
                      (dup(kpe_rot.reshape(N_LAT_TOK, QK_ROPE)), 0, 0), (kv, N_CTX_TOK, n_nope),
                      ((q, N_CTX_TOK, n_nope), (kv_ctx, 0, 0), (kp_ctx, 0, 0), (kv_ctx, 0, n_nope), PAST_LEN),
                      nb=DEC_BATCH, tq_len=DEC_SEQ, out_rows=N_LAT_TOK)
    s_zero = jnp.zeros((BATCH, 2, D_HEADS, D_HEAD_SIZE, D_HEAD_SIZE), F32)
    rw_c, s_c = _rwkv_group(rw[:N_CTX_TOK], g_lo[:N_CTX_TOK], s_zero, i, rp, BATCH, 4)
    rw_l, _ = _rwkv_group(rw[N_CTX_TOK:], g_lo[N_CTX_TOK:], state[:, i].astype(F32), i, rp, DEC_BATCH, 2)
    cat = jnp.concatenate([jnp.concatenate([att_c, att_l], axis=0),
                           jnp.concatenate([rw_c, rw_l], axis=0)], axis=-1).astype(BF16)
    x = _mm(cat, w_out, (i,), res=x, gate=gate, name="odd_out")
    new_ckv = c_kv[:N_CTX_TOK].reshape(BATCH, SEQ, KV_LORA)
    new_kpe = k_pe[:N_CTX_TOK].reshape(BATCH, SEQ, QK_ROPE)
    return x, new_ckv, new_kpe, s_c


def kernel(x_prompt, x_sample, c, c_ctx, cache_attn_k, cache_attn_v, cache_mla_ckv, cache_mla_kpe,
           state_rwkv, w_mod, b_mod, norm_mix, norm_ffn, norm_final, even_w_in, even_w_out, attn_sink,
           conv_w, odd_w_in, odd_w_out, mla_q_norm, mla_kv_norm, mla_w_uq, mla_w_ukv, rwkv_mu, rwkv_w0,
           rwkv_w_up, rwkv_a0, rwkv_a_up, rwkv_k_k, rwkv_k_a, rwkv_r_k, rwkv_g_up, rwkv_ln_w, rwkv_ln_b,
           router_group_w, router_group_b, router_expert_w, router_expert_b, expert_w_gate, expert_w_up,
           expert_w_down):
    d = D_MODEL
    x = jnp.concatenate([x_prompt.reshape(N_CTX_TOK, d), x_sample.reshape(N_LAT_TOK, d)], axis=0)
    cvec = jax.nn.silu(jnp.concatenate([c_ctx[None], c], axis=0))
    cvec = jnp.pad(cvec, ((0, 8 - N_MOD_GROUPS), (0, 0)))
    mp = dict(q_norm=mla_q_norm, kv_norm=mla_kv_norm, w_uq=mla_w_uq, w_ukv=mla_w_ukv)
    rp = dict(mu=rwkv_mu, w0=rwkv_w0, w_up=rwkv_w_up, a0=rwkv_a0, a_up=rwkv_a_up, k_k=rwkv_k_k,
              k_a=rwkv_k_a, r_k=rwkv_r_k, g_up=rwkv_g_up, ln_w=rwkv_ln_w, ln_b=rwkv_ln_b)
    new_k, new_v, new_ckv, new_kpe, new_s = [], [], [], [], []
    for l in range(DEPTH):
        mod = _mm(cvec, w_mod, (l,), tn=2048, name="modulation")[:N_MOD_GROUPS] + b_mod[l]
        shift_mix, scale_mix, gate_mix, shift_ffn, scale_ffn, gate_ffn = jnp.split(mod, 6, axis=-1)
        (h,) = _normmod(x, norm_mix[l], shift_mix, scale_mix, (BF16,))
        i = l // 2
        if l % 2 == 0:
            x, k_c, v_c = _even_mixer(h, i, cache_attn_k, cache_attn_v, even_w_in, even_w_out, attn_sink,
                                      conv_w, x, gate_mix)
            new_k.append(k_c)
            new_v.append(v_c)
        else:
            x, ckv_c, kpe_c, s_c = _odd_mixer(h, i, cache_mla_ckv, cache_mla_kpe, state_rwkv, odd_w_in,
                                              odd_w_out, mp, rp, x, gate_mix)
            new_ckv.append(ckv_c)
            new_kpe.append(kpe_c)
            new_s.append(s_c)
        h_f32, h_bf16 = _normmod(x, norm_ffn[l], shift_ffn, scale_ffn, (F32, BF16))
        w_router = jnp.pad(jnp.concatenate([router_group_w[l], router_expert_w[l]], axis=1),
                           ((0, 0), (0, LANES - N_GROUPS - N_EXPERTS)))
        b_router = jnp.concatenate([router_group_b[l], router_expert_b[l]])
        y = _moe(l, h_f32, h_bf16, w_router, b_router, expert_w_gate, expert_w_up, expert_w_down)
        gate_rows = jnp.concatenate([jnp.broadcast_to(gate_ffn[0], (N_CTX_TOK, d))] +
                                    [jnp.broadcast_to(gate_ffn[1 + b], (DEC_SEQ, d)) for b in range(DEC_BATCH)],
                                    axis=0)
        x = x + gate_rows * y
    zeros = jnp.zeros((N_MOD_GROUPS, d), F32)
    (y_all,) = _normmod(x, norm_final, zeros, zeros, (F32,))
    return (y_all[:N_CTX_TOK].reshape(BATCH, SEQ, d), y_all[N_CTX_TOK:].reshape(DEC_BATCH, DEC_SEQ, d),
            jnp.stack(new_k, axis=1), jnp.stack(new_v, axis=1), jnp.stack(new_ckv, axis=1),
            jnp.stack(new_kpe, axis=1), jnp.stack(new_s, axis=1))
```

```python
import functools

import jax
import jax.numpy as jnp
from jax import lax
from jax.experimental import pallas as pl
from jax.experimental.pallas import tpu as pltpu

F32 = jnp.float32
BF16 = jnp.bfloat16
HIGHEST = lax.Precision.HIGHEST

D_MODEL = 4096
BATCH = 16
SEQ = 256
DEPTH = 4
DEC_BATCH = 2
DEC_SEQ = 1024
PAST_LEN = 512
GRID_W = 64
WINDOW = 128
ROPE_BASE = 10000.0
NORM_EPS = 1e-6
NEG_INF = -1e30

HEAD_DIM = 128
A_HEADS = 16
A_KV_HEADS = 4
A_GROUP = A_HEADS // A_KV_HEADS
A_Q_W = A_HEADS * HEAD_DIM
A_KV_W = A_KV_HEADS * HEAD_DIM
B_WIDTH = D_MODEL // 2
C_HEADS = 16
Q_LORA = D_MODEL // 4
KV_LORA = D_MODEL // 8
QK_NOPE = 128
QK_ROPE = 64
V_DIM = 128
C_OUT_W = C_HEADS * V_DIM
D_WIDTH = D_MODEL // 2
D_HEAD_SIZE = 64
D_HEADS = D_WIDTH // D_HEAD_SIZE
DECAY_LORA = 64
AAA_LORA = 64
GATE_LORA = 256
GN_EPS = 64e-5
SHIFT_W = 3 * D_WIDTH + DECAY_LORA + AAA_LORA
N_GROUPS = 8
EXPERTS_PER_GROUP = 8
N_EXPERTS = N_GROUPS * EXPERTS_PER_GROUP
TOP_K = 2
EXPERT_FF = 512

N_CTX_TOK = BATCH * SEQ
N_LAT_TOK = DEC_BATCH * DEC_SEQ
N_TOK = N_CTX_TOK + N_LAT_TOK
N_MOD_GROUPS = 1 + DEC_BATCH

LANES = 128
VMEM_LIMIT_BYTES = 56 * 1024 * 1024
MOE_ROWS = 256
MOE_FF_TILE = 256
MOE_TOK_BITS = 13
MOE_TOK_MASK = (1 << MOE_TOK_BITS) - 1
RWKV_CHUNK = 32
RWKV_PAIR = 2 * D_HEAD_SIZE
N_PAIRS = D_WIDTH // RWKV_PAIR


def _cparams(*sem):
    return pltpu.CompilerParams(dimension_semantics=sem, vmem_limit_bytes=VMEM_LIMIT_BYTES)


def _row_group(i, tm):
    nc = N_CTX_TOK // tm
    nl = DEC_SEQ // tm
    return jnp.where(i < nc, 0, 1 + (i - nc) // nl)


def _normmod_kernel(*refs, has_res, emit_x):
    if has_res:
        x_ref, y2_ref, gate_ref, g_ref, shift_ref, scale_ref, *out_refs = refs
        d = x_ref.shape[1]
        x = x_ref[...] + gate_ref[0] * (y2_ref[:, :d] + y2_ref[:, d:])
    else:
        x_ref, g_ref, shift_ref, scale_ref, *out_refs = refs
        x = x_ref[...]
    if emit_x:
        out_refs[0][...] = x
        out_refs = out_refs[1:]
    y = x * lax.rsqrt(jnp.mean(x * x, axis=-1, keepdims=True) + NORM_EPS)
    y = y * g_ref[...]
    y = y * (1 + scale_ref[0]) + shift_ref[0]
    for o in out_refs:
        o[...] = y.astype(o.dtype)


def _normmod(x, g, shift, scale, out_dtypes, res=None, emit_x=False, tm=128):
    n, d = x.shape
    grp = lambda i: (_row_group(i, tm), 0, 0)
    row = pl.BlockSpec((tm, d), lambda i: (i, 0))
    vec = pl.BlockSpec((1, 1, d), grp)
    in_specs, args = [row], [x]
    if res is not None:
        y2, gate = res
        in_specs += [pl.BlockSpec((tm, 2 * d), lambda i: (i, 0)), vec]
        args += [y2.reshape(-1, 2 * d), gate.reshape(-1, 1, d)]
    in_specs += [pl.BlockSpec((1, d), lambda i: (0, 0)), vec, vec]
    args += [g.reshape(1, d), shift.reshape(-1, 1, d), scale.reshape(-1, 1, d)]
    out_dtypes = ((F32,) if emit_x else ()) + tuple(out_dtypes)
    return pl.pallas_call(
        functools.partial(_normmod_kernel, has_res=res is not None, emit_x=emit_x),
        grid=(n // tm,),
        in_specs=in_specs,
        out_specs=[row for _ in out_dtypes],
        out_shape=[jax.ShapeDtypeStruct((n, d), dt) for dt in out_dtypes],
        compiler_params=_cparams("parallel"),
        name="normmod",
    )(*args)


def _mm_kernel(*refs, nk, has_res, precision):
    if has_res:
        a_ref, w_ref, res_ref, gate_ref, o_ref, acc_ref = refs
    else:
        a_ref, w_ref, o_ref, acc_ref = refs
    k = pl.program_id(2)
    a = a_ref[...]
    w = w_ref[...]
    if precision is None:
        a = a.astype(BF16)
        w = w.astype(BF16)
    part = jnp.dot(a, w, preferred_element_type=F32, precision=precision)

    def finish(acc):
        if has_res:
            acc = res_ref[...] + gate_ref[0] * acc
        o_ref[...] = acc.astype(o_ref.dtype)

    if nk == 1:
        finish(part)
    else:
        @pl.when(k == 0)
        def _():
            acc_ref[...] = part

        @pl.when(k > 0)
        def _():
            acc_ref[...] += part

        @pl.when(k == nk - 1)
        def _():
            finish(acc_ref[...])


def _mm(a, w, w_idx=(), *, out_dtype=F32, res=None, gate=None, precision=None,
        tm=1024, tn=1024, tk=1024, name="mm"):
    m, kdim = a.shape
    n = w.shape[-1]
    assert w.shape[-2] == kdim
    tm, tn, tk = min(tm, m), min(tn, n), min(tk, kdim)
    assert m % tm == 0 and kdim % tk == 0
    nk = kdim // tk
    lead = tuple(w_idx)
    in_specs = [pl.BlockSpec((tm, tk), lambda i, j, k: (i, k)),
                pl.BlockSpec((None,) * len(lead) + (tk, tn), lambda i, j, k: lead + (k, j))]
    args = [a, w]
    has_res = res is not None
    if has_res:
        in_specs += [pl.BlockSpec((tm, tn), lambda i, j, k: (i, j)),
                     pl.BlockSpec((1, 1, tn), lambda i, j, k: (_row_group(i, tm), 0, j))]
        args += [res, gate.reshape(N_MOD_GROUPS, 1, n)]
    return pl.pallas_call(
        functools.partial(_mm_kernel, nk=nk, has_res=has_res, precision=precision),
        grid=(m // tm, pl.cdiv(n, tn), nk),
        in_specs=in_specs,
        out_specs=pl.BlockSpec((tm, tn), lambda i, j, k: (i, j)),
        out_shape=jax.ShapeDtypeStruct((m, n), out_dtype),
        scratch_shapes=[pltpu.VMEM((tm, tn), F32)],
        compiler_params=_cparams("parallel", "parallel", "arbitrary"),
        name=name,
    )(*args)


def _dot_nt(a, b, precision=None):
    return lax.dot_general(a, b, (((1,), (1,)), ((), ())), preferred_element_type=F32,
                           precision=precision)


def _softmax_read(scores, values, sink_col):
    m = scores[0].max(axis=-1, keepdims=True)
    for s in scores[1:]:
        m = jnp.maximum(m, s.max(axis=-1, keepdims=True))
    if sink_col is not None:
        m = jnp.maximum(m, sink_col)
    den = jnp.exp(sink_col - m) if sink_col is not None else 0.0
    acc = None
    for s, v in zip(scores, values):
        p = jnp.exp(s - m)
        den = den + p.sum(axis=-1, keepdims=True)
        pv = jnp.dot(p.astype(BF16), v, preferred_element_type=F32)
        acc = pv if acc is None else acc + pv
    return acc / den


def _gqa_kernel(*refs, tq, has2, band):
    if has2:
        sink_ref, q1_ref, k1_ref, v1_ref, q2_ref, k2_ref, v2_ref, o_ref = refs
    else:
        sink_ref, q1_ref, k1_ref, v1_ref, o_ref = refs
    h = pl.program_id(1)
    qi = pl.program_id(2)
    scale = HEAD_DIM ** -0.5

    def stack(q_ref):
        q = q_ref[...]
        return jnp.concatenate([q[:, g * HEAD_DIM:(g + 1) * HEAD_DIM] for g in range(A_GROUP)],
                               axis=0).astype(BF16)

    s1 = _dot_nt(stack(q1_ref), k1_ref[...].astype(BF16)) * scale
    if band:
        qpos = qi * tq + lax.broadcasted_iota(jnp.int32, s1.shape, 0) % tq
        kpos = lax.broadcasted_iota(jnp.int32, s1.shape, 1)
        s1 = jnp.where(jnp.abs(kpos - qpos) <= WINDOW, s1, NEG_INF)
    scores, values = [s1], [v1_ref[...].astype(BF16)]
    if has2:
        scores.append(_dot_nt(stack(q2_ref), k2_ref[...].astype(BF16)) * scale)
        values.append(v2_ref[...].astype(BF16))
    sink_col = jnp.concatenate([jnp.full((tq, 1), sink_ref[h * A_GROUP + g], F32)
                                for g in range(A_GROUP)], axis=0)
    out = _softmax_read(scores, values, sink_col)
    for g in range(A_GROUP):
        o_ref[:, g * HEAD_DIM:(g + 1) * HEAD_DIM] = out[g * tq:(g + 1) * tq]


def _tok_spec(rows, width, row0, rows_per_batch, col0, per_q):
    assert row0 % rows == 0 and rows_per_batch % rows == 0 and col0 % width == 0
    r0, rb, c0 = row0 // rows, rows_per_batch // rows, col0 // width
    if per_q:
        return pl.BlockSpec((rows, width), lambda b, h, qi: (r0 + b * rb + qi, c0 + h))
    return pl.BlockSpec((rows, width), lambda b, h, qi: (r0 + b * rb, c0 + h))


def _gqa_attn(sink, q1, k1, v1, seg2=None, *, nb, tq_len, band, out_rows, tq=128):
    gw = A_GROUP * HEAD_DIM
    qspec = lambda t: _tok_spec(tq, gw, t[1], tq_len, t[2], True)
    kspec = lambda t, tk: _tok_spec(tk, HEAD_DIM, t[1], tk, t[2], False)
    in_specs = [pl.BlockSpec(memory_space=pltpu.SMEM), qspec(q1), kspec(k1, tq_len), kspec(v1, tq_len)]
    args = [sink, q1[0], k1[0], v1[0]]
    if seg2 is not None:
        q2, k2, v2, tk2 = seg2
        in_specs += [qspec(q2), kspec(k2, tk2), kspec(v2, tk2)]
        args += [q2[0], k2[0], v2[0]]
    return pl.pallas_call(
        functools.partial(_gqa_kernel, tq=tq, has2=seg2 is not None, band=band),
        grid=(nb, A_KV_HEADS, tq_len // tq),
        in_specs=in_specs,
        out_specs=_tok_spec(tq, gw, 0, tq_len, 0, True),
        out_shape=jax.ShapeDtypeStruct((out_rows, A_Q_W), F32),
        compiler_params=_cparams("parallel", "parallel", "arbitrary"),
        name="gqa_attn",
    )(*args)


def _mla_kernel(*refs, has2):
    if has2:
        qn_ref, qp1_ref, kn1_ref, kp1_ref, v1_ref, qp2_ref, kn2_ref, kp2_ref, v2_ref, o_ref = refs
    else:
        qn_ref, qp1_ref, kn1_ref, kp1_ref, v1_ref, o_ref = refs
    scale = (QK_NOPE + QK_ROPE) ** -0.5
    lane = lax.broadcasted_iota(jnp.int32, qp1_ref.shape, 1)
    for hh in range(2):
        cols = slice(hh * QK_NOPE, (hh + 1) * QK_NOPE)
        mine = (lane < QK_ROPE) if hh == 0 else (lane >= QK_ROPE)
        qn = qn_ref[:, cols]

        def score(qp_ref, kn_ref, kp_ref):
            q = jnp.concatenate([qn, jnp.where(mine, qp_ref[...], 0.0)], axis=1).astype(BF16)
            k = jnp.concatenate([kn_ref[:, cols], kp_ref[...]], axis=1).astype(BF16)
            return _dot_nt(q, k) * scale

        scores = [score(qp1_ref, kn1_ref, kp1_ref)]
        values = [v1_ref[:, cols].astype(BF16)]
        if has2:
            scores.append(score(qp2_ref, kn2_ref, kp2_ref))
            values.append(v2_ref[:, cols].astype(BF16))
        o_ref[:, cols] = _softmax_read(scores, values, None)


def _mla_attn(qn, qp1, kn1, kp1, v1, seg2=None, *, nb, tq_len, out_rows, tq=256):
    pw = 2 * QK_NOPE
    qspec = lambda t, w: _tok_spec(tq, w, t[1], tq_len, t[2], True)
    kspec = lambda t, w, tk: _tok_spec(tk, w, t[1], tk, t[2], False)
    kpspec = lambda t, tk: pl.BlockSpec((tk, LANES), lambda b, h, qi: (t[1] // tk + b, 0))
    in_specs = [qspec(qn, pw), qspec(qp1, LANES), kspec(kn1, pw, tq_len), kpspec(kp1, tq_len),
                kspec(v1, pw, tq_len)]
    args = [qn[0], qp1[0], kn1[0], kp1[0], v1[0]]
    if seg2 is not None:
        qp2, kn2, kp2, v2, tk2 = seg2
        in_specs += [qspec(qp2, LANES), kspec(kn2, pw, tk2), kpspec(kp2, tk2), kspec(v2, pw, tk2)]
        args += [qp2[0], kn2[0], kp2[0], v2[0]]
    return pl.pallas_call(
        functools.partial(_mla_kernel, has2=seg2 is not None),
        grid=(nb, C_HEADS // 2, tq_len // tq),
        in_specs=in_specs,
        out_specs=_tok_spec(tq, pw, 0, tq_len, 0, True),
        out_shape=jax.ShapeDtypeStruct((out_rows, C_OUT_W), F32),
        compiler_params=_cparams("parallel", "parallel", "arbitrary"),
        name="mla_attn",
    )(*args)


_NN = (((1,), (0,)), ((), ()))
_NT = (((1,), (1,)), ((), ()))
_TN = (((0,), (0,)), ((), ()))


def _bdot(a, b, dims=_NN):
    return lax.dot_general(a.astype(BF16), b.astype(BF16), dims, preferred_element_type=F32)


def _bf16_parts(x, n):
    parts = []
    for _ in range(n - 1):
        hi = x.astype(BF16)
        parts.append(hi)
        x = x - hi.astype(F32)
    parts.append(x.astype(BF16))
    return parts


def _rwkv_kernel(r_ref, lw_ref, k_ref, v_ref, a_ref, b_ref, s0_ref, y_ref, s_ref,
                 p_scr, q_scr, rw_scr, y0_scr, *, seq, pairs, reverse, unroll):
    c = RWKV_CHUNK
    nchunk = seq // c
    gl = 2 * RWKV_PAIR
    hg = gl // D_HEAD_SIZE
    cs = hg * c
    width = pairs * RWKV_PAIR
    row = lax.broadcasted_iota(jnp.int32, (c, c), 0)
    col = lax.broadcasted_iota(jnp.int32, (c, c), 1)
    tri = ((col >= row) if reverse else (col <= row)).astype(BF16)
    last = 0 if reverse else c - 1
    head_of_lane = lax.broadcasted_iota(jnp.int32, (c, gl), 1) // D_HEAD_SIZE
    grow = lax.broadcasted_iota(jnp.int32, (cs, 2 * cs), 0)
    gcol = lax.broadcasted_iota(jnp.int32, (cs, 2 * cs), 1)
    gcol_in = jnp.where(gcol >= cs, gcol - cs, gcol)
    before = (gcol_in > grow) if reverse else (gcol_in < grow)
    upto = (gcol_in >= grow) if reverse else (gcol_in <= grow)
    eye = (lax.broadcasted_iota(jnp.int32, (cs, cs), 0) ==
           lax.broadcasted_iota(jnp.int32, (cs, cs), 1)).astype(F32)
    eye_pair = (lax.broadcasted_iota(jnp.int32, (RWKV_PAIR, RWKV_PAIR), 0) ==
                lax.broadcasted_iota(jnp.int32, (RWKV_PAIR, RWKV_PAIR), 1))
    n_double = c.bit_length() - 2

    def stacked(x):
        return jnp.concatenate([jnp.where(head_of_lane == h, x, 0.0) for h in range(hg)], axis=0)

    def local(step, carry):
        items = []
        for ui in range(unroll):
            cidx = step * unroll + ui
            rows = pl.ds(pl.multiple_of(cidx * c, c), c)
            r, lw, k = r_ref[0, rows, :], lw_ref[0, rows, :], k_ref[0, rows, :]
            v, a, b = v_ref[0, rows, :], a_ref[0, rows, :], b_ref[0, rows, :]
            cum = jnp.dot(tri, jnp.concatenate(_bf16_parts(lw, 3), axis=1), preferred_element_type=F32)
            lg = cum[:, :width] + (cum[:, width:2 * width] + cum[:, 2 * width:])
            lg_end = lg[last:last + 1]
            g_inv = jnp.exp(-lg)
            g_end = jnp.exp(lg_end - lg)
            g_chunk = jnp.exp(lg_end)
            full = (a * jnp.exp(lg - lw), r * jnp.exp(lg), b * g_inv, k * g_inv, v, b * g_end, k * g_end)
            for g in range(pairs // 2):
                ops = tuple(stacked(x[:, g * gl:(g + 1) * gl]) for x in full)
                items.append((cidx, g, g_chunk) + ops)
        n = range(len(items))
        at, rt, bt, kt, vs, bh, kh = ([it[3 + j] for it in items] for j in range(7))
        gram = [_bdot(jnp.concatenate([at[i], rt[i]], axis=0), jnp.concatenate([bt[i], kt[i]], axis=0), _NT)
                for i in n]
        g_u = [jnp.where(before, gram[i][:cs], 0.0) for i in n]
        g_y = [jnp.where(upto, gram[i][cs:], 0.0) for i in n]
        low = [g_u[i][:, :cs] for i in n]
        tinv = [eye + low[i] for i in n]
        pw = [_bdot(low[i], low[i]) for i in n]
        c0 = [_bdot(g_u[i][:, cs:], vs[i]) for i in n]
        for it in range(n_double):
            if it < n_double - 1:
                both = [_bdot(jnp.concatenate([pw[i], tinv[i]], axis=0), pw[i]) for i in n]
                pw = [both[i][:cs] for i in n]
                tinv = [tinv[i] + both[i][cs:] for i in n]
            else:
                tinv = [tinv[i] + _bdot(tinv[i], pw[i]) for i in n]
        wu = [_bdot(tinv[i], jnp.concatenate([at[i], c0[i]], axis=1)) for i in n]
        pmat = [_bdot(wu[i][:, :gl], bh[i], _TN) for i in n]
        qmat = [_bdot(jnp.concatenate([wu[i][:, gl:], vs[i]], axis=0),
                      jnp.concatenate([bh[i], kh[i]], axis=0), _TN) for i in n]
        ry = [_bdot(g_y[i], jnp.concatenate([wu[i], jnp.concatenate([jnp.zeros_like(vs[i]), vs[i]], axis=1)],
                                            axis=0)) for i in n]
        for i in n:
            cidx, g, g_chunk = items[i][:3]
            rw, y0 = rt[i] + ry[i][:, :gl], ry[i][:, gl:]
            for j in range(2):
                p = 2 * g + j
                lanes = slice(j * RWKV_PAIR, (j + 1) * RWKV_PAIR)
                head_rows = slice(2 * j * c, (2 * j + 2) * c)
                decay = g_chunk[:, p * RWKV_PAIR:(p + 1) * RWKV_PAIR]
                p_scr[cidx, p] = jnp.where(eye_pair, decay, 0.0) + pmat[i][lanes, lanes]
                q_scr[cidx, p] = qmat[i][lanes, lanes]
                rw_scr[cidx, p] = rw[head_rows, lanes]
                y0_scr[cidx, p] = y0[head_rows, lanes]
        return carry

    lax.fori_loop(0, nchunk // unroll, local, 0)
    s_ref[0] = s0_ref[0]

    def scan(ci, carry):
        cidx = (nchunk - 1 - ci) if reverse else ci
        rows = pl.ds(pl.multiple_of(cidx * c, c), c)
        s = [s_ref[0, p] for p in range(pairs)]
        ys = [_bdot(rw_scr[cidx, p], s[p], _NT) for p in range(pairs)]
        s_new = [_bdot(s[p], p_scr[cidx, p]) for p in range(pairs)]
        for p in range(pairs):
            y = ys[p] + y0_scr[cidx, p]
            y_ref[0, rows, p * RWKV_PAIR:(p + 1) * RWKV_PAIR] = y[:c] + y[c:]
            s_ref[0, p] = s_new[p] + q_scr[cidx, p]
        return carry

    lax.fori_loop(0, nchunk, scan, 0)


def _rwkv_scan(r, lw, k, v, a, b, s0, *, reverse, pairs, unroll):
    nb, seq, _ = r.shape
    assert 4 * RWKV_CHUNK == RWKV_PAIR and seq % (RWKV_CHUNK * unroll) == 0
    assert pairs % 2 == 0 and N_PAIRS % pairs == 0
    nchunk = seq // RWKV_CHUNK
    tok = pl.BlockSpec((1, seq, pairs * RWKV_PAIR), lambda bi, pi: (bi, 0, pi))
    st = pl.BlockSpec((1, pairs, RWKV_PAIR, RWKV_PAIR), lambda bi, pi: (bi, pi, 0, 0))
    pair_mat = pltpu.VMEM((nchunk, pairs, RWKV_PAIR, RWKV_PAIR), F32)
    pair_rows = pltpu.VMEM((nchunk, pairs, 2 * RWKV_CHUNK, RWKV_PAIR), F32)
    return pl.pallas_call(
        functools.partial(_rwkv_kernel, seq=seq, pairs=pairs, reverse=reverse, unroll=unroll),
        grid=(nb, N_PAIRS // pairs),
        in_specs=[tok] * 6 + [st],
        out_specs=[tok, st],
        out_shape=[jax.ShapeDtypeStruct(r.shape, F32), jax.ShapeDtypeStruct(s0.shape, F32)],
        scratch_shapes=[pair_mat, pair_mat, pair_rows, pair_rows],
        compiler_params=_cparams("parallel", "parallel"),
        name="rwkv_rev" if reverse else "rwkv_fwd",
    )(r, lw, k, v, a, b, s0)


def _moe_kernel(be_ref, nv_ref, idx_ref, w_ref, h_hbm, wg_ref, wu_ref, wd_ref, y_hbm, xbuf, acc, gsem, ssem):
    i = pl.program_id(0)
    f = pl.program_id(1)
    n_used = nv_ref[0]
    cur = i % 2

    def row_in(blk, r, dst):
        tok = idx_ref[blk * MOE_ROWS + r] & MOE_TOK_MASK
        return pltpu.make_async_copy(h_hbm.at[pl.ds(tok, 1)], xbuf.at[dst, pl.ds(r, 1)], gsem.at[dst])

    def row_out(r):
        slot = idx_ref[i * MOE_ROWS + r] >> MOE_TOK_BITS
        return pltpu.make_async_copy(acc.at[pl.ds(r, 1)], y_hbm.at[pl.ds(slot, 1)], ssem)

    def gather(blk, dst):
        def body(r, carry):
            row_in(blk, r, dst).start()
            return carry
        lax.fori_loop(0, MOE_ROWS, body, 0)

    @pl.when((f == 0) & (i == 0))
    def _():
        n_slots = y_hbm.shape[0] - MOE_ROWS
        acc[...] = jnp.zeros_like(acc)
        tail = pltpu.make_async_copy(acc, y_hbm.at[pl.ds(n_slots, MOE_ROWS)], ssem)
        tail.start()
        tail.wait()

    @pl.when((f == 0) & (i == 0) & (n_used > 0))
    def _():
        gather(0, 0)

    @pl.when((f == 0) & (i < n_used))
    def _():
        pltpu.make_async_copy(h_hbm.at[pl.ds(0, MOE_ROWS)], xbuf.at[cur], gsem.at[cur]).wait()

        @pl.when(i + 1 < n_used)
        def _():
            gather(i + 1, 1 - cur)

    @pl.when(i < n_used)
    def _():
        x = xbuf[cur].astype(BF16)
        g = jnp.dot(x, wg_ref[...].astype(BF16), preferred_element_type=F32)
        u = jnp.dot(x, wu_ref[...].astype(BF16), preferred_element_type=F32)
        h = (g * jax.nn.sigmoid(g)) * u
        y = jnp.dot(h.astype(BF16), wd_ref[...].astype(BF16), preferred_element_type=F32)

        @pl.when(f == 0)
        def _():
            acc[...] = y

        @pl.when(f > 0)
        def _():
            acc[...] += y

        @pl.when(f == pl.num_programs(1) - 1)
        def _():
            acc[...] = acc[...] * w_ref[...]

            def body(r, carry):
                row_out(r).start()
                return carry
            lax.fori_loop(0, MOE_ROWS, body, 0)
            pltpu.make_async_copy(acc, y_hbm.at[pl.ds(0, MOE_ROWS)], ssem).wait()


def _moe_experts(layer, block_e, n_used, row_idx, row_w, h, w_gate, w_up, w_down):
    n_blocks = block_e.shape[0]
    n_tok, d = h.shape
    wspec = pl.BlockSpec((None, None, d, MOE_FF_TILE), lambda i, f, be, nv, ix: (layer, be[i], 0, f))
    return pl.pallas_call(
        _moe_kernel,
        grid_spec=pltpu.PrefetchScalarGridSpec(
            num_scalar_prefetch=3,
            grid=(n_blocks, EXPERT_FF // MOE_FF_TILE),
            in_specs=[pl.BlockSpec((MOE_ROWS, 1), lambda i, f, be, nv, ix: (i, 0)),
                      pl.BlockSpec(memory_space=pl.ANY),
                      wspec, wspec,
                      pl.BlockSpec((None, None, MOE_FF_TILE, d), lambda i, f, be, nv, ix: (layer, be[i], f, 0))],
            out_specs=pl.BlockSpec(memory_space=pl.ANY),
            scratch_shapes=[pltpu.VMEM((2, MOE_ROWS, d), F32), pltpu.VMEM((MOE_ROWS, d), F32),
                            pltpu.SemaphoreType.DMA((2,)), pltpu.SemaphoreType.DMA(())]),
        out_shape=jax.ShapeDtypeStruct((n_tok * TOP_K + MOE_ROWS, d), F32),
        compiler_params=_cparams("arbitrary", "arbitrary"),
        name="moe_experts",
    )(block_e, n_used, row_idx, row_w, h, w_gate, w_up, w_down)


def _moe(layer, h_f32, w_router, b_router, w_gate, w_up, w_down):
    n_tok = h_f32.shape[0]
    logits = _mm(h_f32, w_router, precision=HIGHEST, tn=LANES, name="router")
    logits = logits[:, :N_GROUPS + N_EXPERTS] + b_router
    gp = jax.nn.softmax(logits[:, :N_GROUPS], axis=-1)
    p_g, g_idx = lax.top_k(gp, 1)
    el = logits[:, N_GROUPS:].reshape(-1, N_GROUPS, EXPERTS_PER_GROUP)
    el = jnp.take_along_axis(el, g_idx[:, :, None], axis=1)[:, 0]
    p_e, e_loc = lax.top_k(jax.nn.softmax(el, axis=-1), TOP_K)
    wts = p_g * p_e / jnp.sum(p_e, axis=-1, keepdims=True)
    idx = g_idx * EXPERTS_PER_GROUP + e_loc

    n_slots = n_tok * TOP_K
    assert n_tok <= MOE_TOK_MASK and n_slots % MOE_ROWS == 0
    flat_e = idx.reshape(-1).astype(jnp.int32)
    order = jnp.argsort(flat_e).astype(jnp.int32)
    sorted_e = flat_e[order]
    counts = jnp.bincount(flat_e, length=N_EXPERTS)
    padded = (counts + MOE_ROWS - 1) // MOE_ROWS * MOE_ROWS
    pad_end = jnp.cumsum(padded)
    pad_start = pad_end - padded
    start = jnp.cumsum(counts) - counts
    dest_sorted = (pad_start[sorted_e] + jnp.arange(n_slots) - start[sorted_e]).astype(jnp.int32)
    n_blocks = n_slots // MOE_ROWS + N_EXPERTS
    n_rows = n_blocks * MOE_ROWS
    pad_idx = (n_slots + jnp.arange(n_rows, dtype=jnp.int32) % MOE_ROWS) << MOE_TOK_BITS
    row_idx = pad_idx.at[dest_sorted].set((order // TOP_K) | (order << MOE_TOK_BITS))
    row_w = jnp.zeros((n_rows,), F32).at[dest_sorted].set(wts.reshape(-1)[order])
    block_e = jnp.minimum(jnp.searchsorted(pad_end, jnp.arange(n_blocks) * MOE_ROWS, side='right'),
                          N_EXPERTS - 1).astype(jnp.int32)
    n_used = (pad_end[-1] // MOE_ROWS).astype(jnp.int32).reshape(1)
    return _moe_experts(layer, block_e, n_used, row_idx, row_w.reshape(n_rows, 1), h_f32, w_gate, w_up, w_down)


def _axial_rope_tables(n_tok, rot_dim):
    t = jnp.arange(n_tok)
    rows = (t // GRID_W).astype(F32)
    cols = (t % GRID_W).astype(F32)
    axis_dim = rot_dim // 2
    inv = ROPE_BASE ** (-jnp.arange(0, axis_dim, 2, dtype=F32) / axis_dim)
    ang_r = rows[:, None] * inv[None, :]
    ang_c = cols[:, None] * inv[None, :]
    return jnp.cos(ang_r), jnp.sin(ang_r), jnp.cos(ang_c), jnp.sin(ang_c)


def _rope_rotate(x, cos, sin):
    half = x.shape[-1] // 2
    c, s = cos[None, :, None, :], sin[None, :, None, :]
    x1, x2 = x[..., :half], x[..., half:]
    return jnp.concatenate([x1 * c - x2 * s, x2 * c + x1 * s], axis=-1)


def _apply_axial_rope(x, tables):
    cos_r, sin_r, cos_c, sin_c = tables
    h = x.shape[-1] // 2
    return jnp.concatenate([_rope_rotate(x[..., :h], cos_r, sin_r),
                            _rope_rotate(x[..., h:], cos_c, sin_c)], axis=-1)


def _short_conv(bg, cg, u, conv_w, nb):
    z = (cg * u).reshape(nb, -1, B_WIDTH)
    zp = jnp.pad(z, ((0, 0), (1, 1), (0, 0)))
    conv = conv_w[0] * zp[:, :-2] + conv_w[1] * zp[:, 1:-1] + conv_w[2] * zp[:, 2:]
    return bg * conv.reshape(-1, B_WIDTH)


def _even_mixer(h, i, cache_k, cache_v, w_in, w_out, sink, conv_w, x, gate):
    p = _mm(h, w_in, (i,), name="even_in")
    o_k, o_v, o_b = A_Q_W, A_Q_W + A_KV_W, A_Q_W + 2 * A_KV_W
    att_c = _gqa_attn(sink[i], (p, 0, 0), (p, 0, o_k), (p, 0, o_v),
                      nb=BATCH, tq_len=SEQ, band=False, out_rows=N_CTX_TOK)
    pl_ = p[N_CTX_TOK:]
    tabs = _axial_rope_tables(DEC_SEQ, HEAD_DIM)
    q_rot = _apply_axial_rope(pl_[:, :A_Q_W].reshape(DEC_BATCH, DEC_SEQ, A_HEADS, HEAD_DIM), tabs)
    k_rot = _apply_axial_rope(pl_[:, o_k:o_v].reshape(DEC_BATCH, DEC_SEQ, A_KV_HEADS, HEAD_DIM), tabs)
    ck = cache_k[:, i].reshape(DEC_BATCH * PAST_LEN, A_KV_W)
    cv = cache_v[:, i].reshape(DEC_BATCH * PAST_LEN, A_KV_W)
    att_l = _gqa_attn(sink[i], (q_rot.reshape(N_LAT_TOK, A_Q_W), 0, 0),
                      (k_rot.reshape(N_LAT_TOK, A_KV_W), 0, 0), (p, N_CTX_TOK, o_v),
                      ((p, N_CTX_TOK, 0), (ck, 0, 0), (cv, 0, 0), PAST_LEN),
                      nb=DEC_BATCH, tq_len=DEC_SEQ, band=True, out_rows=N_LAT_TOK)
    bg, cg, u = (p[:, o_b + j * B_WIDTH:o_b + (j + 1) * B_WIDTH] for j in range(3))
    conv = jnp.concatenate([_short_conv(bg[:N_CTX_TOK], cg[:N_CTX_TOK], u[:N_CTX_TOK], conv_w[i], BATCH),
                            _short_conv(bg[N_CTX_TOK:], cg[N_CTX_TOK:], u[N_CTX_TOK:], conv_w[i], DEC_BATCH)],
                           axis=0)
    cat = jnp.concatenate([jnp.concatenate([att_c, att_l], axis=0), conv], axis=-1).astype(BF16)
    x = _mm(cat, w_out, (i,), res=x, gate=gate, name="even_out")
    new_k = p[:N_CTX_TOK, o_k:o_v].reshape(BATCH, SEQ, A_KV_HEADS, HEAD_DIM)
    new_v = p[:N_CTX_TOK, o_v:o_b].reshape(BATCH, SEQ, A_KV_HEADS, HEAD_DIM)
    return x, new_k, new_v


def _rms(x, g):
    return x * lax.rsqrt(jnp.mean(x * x, axis=-1, keepdims=True) + NORM_EPS) * g


def _rwkv_group(rw, g_lo, s0, i, rp, nb, pairs):
    seq = rw.shape[0] // nb
    rw3 = rw.reshape(nb, seq, SHIFT_W)
    outs, finals = [], []
    for d in range(2):
        if d == 0:
            nbr = jnp.pad(rw3, ((0, 0), (1, 0), (0, 0)))[:, :-1]
        else:
            nbr = jnp.pad(rw3, ((0, 0), (0, 1), (0, 0)))[:, 1:]
        z = (rw3 + rp['mu'][i, d] * (nbr - rw3)).reshape(nb * seq, SHIFT_W)
        r, k, v = (z[:, j * D_WIDTH:(j + 1) * D_WIDTH] for j in range(3))
        w_lo = z[:, 3 * D_WIDTH:3 * D_WIDTH + DECAY_LORA]
        a_lo = z[:, 3 * D_WIDTH + DECAY_LORA:]
        w_raw = rp['w0'][i, d] + _mm(jnp.tanh(w_lo), rp['w_up'], (i, d), name="rwkv_decay")
        lw = -jnp.exp(-jax.nn.softplus(-w_raw) - 0.5)
        a = jax.nn.sigmoid(rp['a0'][i, d] + _mm(a_lo, rp['a_up'], (i, d), name="rwkv_aaa"))
        heads = lambda t: t.reshape(nb * seq, D_HEADS, D_HEAD_SIZE)
        kk = heads(k * rp['k_k'][i, d])
        kk = kk / jnp.maximum(jnp.sqrt(jnp.sum(kk * kk, axis=-1, keepdims=True)), 1e-12)
        kk = kk.reshape(nb * seq, D_WIDTH)
        k = k * (1 + (a - 1) * rp['k_a'][i, d])
        to3 = lambda t: t.reshape(nb, seq, D_WIDTH)
        sd = s0[:, d].reshape(nb, N_PAIRS, 2, D_HEAD_SIZE, D_HEAD_SIZE)
        eye2 = jnp.eye(2, dtype=F32)
        s_bd = jnp.einsum('bphij,hg->bphigj', sd, eye2).reshape(nb, N_PAIRS, RWKV_PAIR, RWKV_PAIR)
        y, s_fin = _rwkv_scan(to3(r), to3(lw), to3(k), to3(v), to3(-kk), to3(kk * a), s_bd,
                              reverse=(d == 1), pairs=pairs, unroll=8 // pairs)
        s_fin = s_fin.reshape(nb, N_PAIRS, 2, D_HEAD_SIZE, 2, D_HEAD_SIZE)
        s_fin = jnp.stack([s_fin[:, :, 0, :, 0], s_fin[:, :, 1, :, 1]], axis=2)
        finals.append(s_fin.reshape(nb, D_HEADS, D_HEAD_SIZE, D_HEAD_SIZE))
        yh = heads(y.reshape(nb * seq, D_WIDTH))
        mu = jnp.mean(yh, axis=-1, keepdims=True)
        var = jnp.mean(jnp.square(yh - mu), axis=-1, keepdims=True)
        yn = ((yh - mu) * lax.rsqrt(var + GN_EPS)).reshape(nb * seq, D_WIDTH) * rp['ln_w'][i] + rp['ln_b'][i]
        bonus = jnp.sum(heads(r) * heads(k) * rp['r_k'][i, d], axis=-1, keepdims=True) * heads(v)
        outs.append(yn + bonus.reshape(nb * seq, D_WIDTH))
    g = _mm(jax.nn.sigmoid(g_lo), rp['g_up'], (i,), name="rwkv_gate")
    return (outs[0] + outs[1]) * g, jnp.stack(finals, axis=1)


def _odd_mixer(h, i, cache_ckv, cache_kpe, state, w_in, w_out, mp, rp, x, gate):
    p = _mm(h, w_in, (i,), name="odd_in")
    o1 = Q_LORA
    o2 = o1 + KV_LORA
    o3 = o2 + QK_ROPE
    o4 = o3 + SHIFT_W
    q_down, kv_down, k_pe, rw, g_lo = p[:, :o1], p[:, o1:o2], p[:, o2:o3], p[:, o3:o4], p[:, o4:]
    w_uq = mp['w_uq'][i].reshape(Q_LORA, C_HEADS, QK_NOPE + QK_ROPE)
    w_uq = jnp.concatenate([w_uq[:, :, :QK_NOPE].reshape(Q_LORA, -1), w_uq[:, :, QK_NOPE:].reshape(Q_LORA, -1)],
                           axis=1)
    w_ukv = mp['w_ukv'][i].reshape(KV_LORA, C_HEADS, QK_NOPE + V_DIM)
    w_ukv = jnp.concatenate([w_ukv[:, :, :QK_NOPE].reshape(KV_LORA, -1), w_ukv[:, :, QK_NOPE:].reshape(KV_LORA, -1)],
                            axis=1)
    q = _mm(_rms(q_down, mp['q_norm'][i]), w_uq, name="mla_uq")
    c_kv = _rms(kv_down, mp['kv_norm'][i])
    kv = _mm(c_kv, w_ukv, name="mla_ukv")
    kv_ctx = _mm(cache_ckv[:, i].reshape(DEC_BATCH * PAST_LEN, KV_LORA), w_ukv, name="mla_ukv_cache")
    n_nope = C_HEADS * QK_NOPE
    dup = lambda t: jnp.concatenate([t, t], axis=-1)
    kp = dup(k_pe)
    att_c = _mla_attn((q, 0, 0), (q, 0, n_nope), (kv, 0, 0), (kp, 0, 0), (kv, 0, n_nope),
                      nb=BATCH, tq_len=SEQ, out_rows=N_CTX_TOK)
    tabs = _axial_rope_tables(DEC_SEQ, QK_ROPE)
    qpe_rot = _apply_axial_rope(q[N_CTX_TOK:, n_nope:].reshape(DEC_BATCH, DEC_SEQ, C_HEADS, QK_ROPE), tabs)
    kpe_rot = _apply_axial_rope(k_pe[N_CTX_TOK:].reshape(DEC_BATCH, DEC_SEQ, 1, QK_ROPE), tabs)
    kp_ctx = dup(cache_kpe[:, i].reshape(DEC_BATCH * PAST_LEN, QK_ROPE))
    att_l = _mla_attn((q, N_CTX_TOK, 0), (qpe_rot.reshape(N_LAT_TOK, -1), 0, 0), (kv, N_CTX_TOK, 0),
                      (dup(kpe_rot.reshape(N_LAT_TOK, QK_ROPE)), 0, 0), (kv, N_CTX_TOK, n_nope),
                      ((q, N_CTX_TOK, n_nope), (kv_ctx, 0, 0), (kp_ctx, 0, 0), (kv_ctx, 0, n_nope), PAST_LEN),
                      nb=DEC_BATCH, tq_len=DEC_SEQ, out_rows=N_LAT_TOK)
    s_zero = jnp.zeros((BATCH, 2, D_HEADS, D_HEAD_SIZE, D_HEAD_SIZE), F32)
    rw_c, s_c = _rwkv_group(rw[:N_CTX_TOK], g_lo[:N_CTX_TOK], s_zero, i, rp, BATCH, 4)
    rw_l, _ = _rwkv_group(rw[N_CTX_TOK:], g_lo[N_CTX_TOK:], state[:, i].astype(F32), i, rp, DEC_BATCH, 2)
    cat = jnp.concatenate([jnp.concatenate([att_c, att_l], axis=0),
                           jnp.concatenate([rw_c, rw_l], axis=0)], axis=-1).astype(BF16)
    x = _mm(cat, w_out, (i,), res=x, gate=gate, name="odd_out")
    new_ckv = c_kv[:N_CTX_TOK].reshape(BATCH, SEQ, KV_LORA)
    new_kpe = k_pe[:N_CTX_TOK].reshape(BATCH, SEQ, QK_ROPE)
    return x, new_ckv, new_kpe, s_c


def kernel(x_prompt, x_sample, c, c_ctx, cache_attn_k, cache_attn_v, cache_mla_ckv, cache_mla_kpe,
           state_rwkv, w_mod, b_mod, norm_mix, norm_ffn, norm_final, even_w_in, even_w_out, attn_sink,
           conv_w, odd_w_in, odd_w_out, mla_q_norm, mla_kv_norm, mla_w_uq, mla_w_ukv, rwkv_mu, rwkv_w0,
           rwkv_w_up, rwkv_a0, rwkv_a_up, rwkv_k_k, rwkv_k_a, rwkv_r_k, rwkv_g_up, rwkv_ln_w, rwkv_ln_b,
           router_group_w, router_group_b, router_expert_w, router_expert_b, expert_w_gate, expert_w_up,
           expert_w_down):
    d = D_MODEL
    x = jnp.concatenate([x_prompt.reshape(N_CTX_TOK, d), x_sample.reshape(N_LAT_TOK, d)], axis=0)
    cvec = jax.nn.silu(jnp.concatenate([c_ctx[None], c], axis=0))
    cvec = jnp.pad(cvec, ((0, 8 - N_MOD_GROUPS), (0, 0)))
    mp = dict(q_norm=mla_q_norm, kv_norm=mla_kv_norm, w_uq=mla_w_uq, w_ukv=mla_w_ukv)
    rp = dict(mu=rwkv_mu, w0=rwkv_w0, w_up=rwkv_w_up, a0=rwkv_a0, a_up=rwkv_a_up, k_k=rwkv_k_k,
              k_a=rwkv_k_a, r_k=rwkv_r_k, g_up=rwkv_g_up, ln_w=rwkv_ln_w, ln_b=rwkv_ln_b)
    new_k, new_v, new_ckv, new_kpe, new_s = [], [], [], [], []
    ffn = None
    for l in range(DEPTH):
        mod = _mm(cvec, w_mod, (l,), tn=2048, name="modulation")[:N_MOD_GROUPS] + b_mod[l]
        shift_mix, scale_mix, gate_mix, shift_ffn, scale_ffn, gate_ffn = jnp.split(mod, 6, axis=-1)
        if ffn is None:
            (h,) = _normmod(x, norm_mix[l], shift_mix, scale_mix, (BF16,))
        else:
            x, h = _normmod(x, norm_mix[l], shift_mix, scale_mix, (BF16,), res=ffn, emit_x=True)
        i = l // 2
        if l % 2 == 0:
            x, k_c, v_c = _even_mixer(h, i, cache_attn_k, cache_attn_v, even_w_in, even_w_out, attn_sink,
                                      conv_w, x, gate_mix)
            new_k.append(k_c)
            new_v.append(v_c)
        else:
            x, ckv_c, kpe_c, s_c = _odd_mixer(h, i, cache_mla_ckv, cache_mla_kpe, state_rwkv, odd_w_in,
                                              odd_w_out, mp, rp, x, gate_mix)
            new_ckv.append(ckv_c)
            new_kpe.append(kpe_c)
            new_s.append(s_c)
        (h_ffn,) = _normmod(x, norm_ffn[l], shift_ffn, scale_ffn, (F32,))
        w_router = jnp.pad(jnp.concatenate([router_group_w[l], router_expert_w[l]], axis=1),
                           ((0, 0), (0, LANES - N_GROUPS - N_EXPERTS)))
        b_router = jnp.concatenate([router_group_b[l], router_expert_b[l]])
        ffn = (_moe(l, h_ffn, w_router, b_router, expert_w_gate, expert_w_up, expert_w_down), gate_ffn)
    zeros = jnp.zeros((N_MOD_GROUPS, d), F32)
    (y_all,) = _normmod(x, norm_final, zeros, zeros, (F32,), res=ffn)
    return (y_all[:N_CTX_TOK].reshape(BATCH, SEQ, d), y_all[N_CTX_TOK:].reshape(DEC_BATCH, DEC_SEQ, d),
            jnp.stack(new_k, axis=1), jnp.stack(new_v, axis=1), jnp.stack(new_ckv, axis=1),
            jnp.stack(new_kpe, axis=1), jnp.stack(new_s, axis=1))
```

```python
import functools

import jax
import jax.numpy as jnp
from jax import lax
from jax.experimental import pallas as pl
from jax.experimental.pallas import tpu as pltpu

F32 = jnp.float32
BF16 = jnp.bfloat16
HIGHEST = lax.Precision.HIGHEST

D_MODEL = 4096
BATCH = 16
SEQ = 256
DEPTH = 4
DEC_BATCH = 2
DEC_SEQ = 1024
PAST_LEN = 512
GRID_W = 64
WINDOW = 128
ROPE_BASE = 10000.0
NORM_EPS = 1e-6
NEG_INF = -1e30

HEAD_DIM = 128
A_HEADS = 16
A_KV_HEADS = 4
A_GROUP = A_HEADS // A_KV_HEADS
A_Q_W = A_HEADS * HEAD_DIM
A_KV_W = A_KV_HEADS * HEAD_DIM
B_WIDTH = D_MODEL // 2
C_HEADS = 16
Q_LORA = D_MODEL // 4
KV_LORA = D_MODEL // 8
QK_NOPE = 128
QK_ROPE = 64
V_DIM = 128
C_OUT_W = C_HEADS * V_DIM
D_WIDTH = D_MODEL // 2
D_HEAD_SIZE = 64
D_HEADS = D_WIDTH // D_HEAD_SIZE
DECAY_LORA = 64
AAA_LORA = 64
GATE_LORA = 256
GN_EPS = 64e-5
SHIFT_W = 3 * D_WIDTH + DECAY_LORA + AAA_LORA
N_GROUPS = 8
EXPERTS_PER_GROUP = 8
N_EXPERTS = N_GROUPS * EXPERTS_PER_GROUP
TOP_K = 2
EXPERT_FF = 512

N_CTX_TOK = BATCH * SEQ
N_LAT_TOK = DEC_BATCH * DEC_SEQ
N_TOK = N_CTX_TOK + N_LAT_TOK
N_MOD_GROUPS = 1 + DEC_BATCH

LANES = 128
VMEM_LIMIT_BYTES = 56 * 1024 * 1024
MOE_ROWS = 256
MOE_TOK_BITS = 13
MOE_TOK_MASK = (1 << MOE_TOK_BITS) - 1
RWKV_CHUNK = 32
RWKV_PAIR = 2 * D_HEAD_SIZE
N_PAIRS = D_WIDTH // RWKV_PAIR


_NN = (((1,), (0,)), ((), ()))
_NT = (((1,), (1,)), ((), ()))
_TN = (((0,), (0,)), ((), ()))


def _cparams(*sem):
    return pltpu.CompilerParams(dimension_semantics=sem, vmem_limit_bytes=VMEM_LIMIT_BYTES)


def _row_group(i, tm):
    nc = N_CTX_TOK // tm
    nl = DEC_SEQ // tm
    return jnp.where(i < nc, 0, 1 + (i - nc) // nl)


def _normmod_kernel(*refs, has_res, emit_x):
    if has_res:
        x_ref, ya_ref, yb_ref, gate_ref, g_ref, shift_ref, scale_ref, *out_refs = refs
        x = x_ref[...] + gate_ref[0] * (ya_ref[...] + yb_ref[...])
    else:
        x_ref, g_ref, shift_ref, scale_ref, *out_refs = refs
        x = x_ref[...]
    if emit_x:
        out_refs[0][...] = x
        out_refs = out_refs[1:]
    y = x * lax.rsqrt(jnp.mean(x * x, axis=-1, keepdims=True) + NORM_EPS)
    y = y * g_ref[...]
    y = y * (1 + scale_ref[0]) + shift_ref[0]
    for o in out_refs:
        o[...] = y.astype(o.dtype)


def _normmod(x, g, shift, scale, out_dtypes, res=None, emit_x=False, tm=128):
    n, d = x.shape
    grp = lambda i: (_row_group(i, tm), 0, 0)
    row = pl.BlockSpec((tm, d), lambda i: (i, 0))
    vec = pl.BlockSpec((1, 1, d), grp)
    in_specs, args = [row], [x]
    if res is not None:
        y2, gate = res
        in_specs += [row, pl.BlockSpec((tm, d), lambda i: (n // tm + i, 0)), vec]
        args += [y2, y2, gate.reshape(-1, 1, d)]
    in_specs += [pl.BlockSpec((1, d), lambda i: (0, 0)), vec, vec]
    args += [g.reshape(1, d), shift.reshape(-1, 1, d), scale.reshape(-1, 1, d)]
    out_dtypes = ((F32,) if emit_x else ()) + tuple(out_dtypes)
    return pl.pallas_call(
        functools.partial(_normmod_kernel, has_res=res is not None, emit_x=emit_x),
        grid=(n // tm,),
        in_specs=in_specs,
        out_specs=[row for _ in out_dtypes],
        out_shape=[jax.ShapeDtypeStruct((n, d), dt) for dt in out_dtypes],
        compiler_params=_cparams("parallel"),
        name="normmod",
    )(*args)


def _mm_kernel(*refs, nk, has_res, precision, w_t):
    if has_res:
        a_ref, w_ref, res_ref, gate_ref, o_ref, acc_ref = refs
    else:
        a_ref, w_ref, o_ref, acc_ref = refs
    k = pl.program_id(2)
    a = a_ref[...]
    w = w_ref[...]
    if precision is None:
        a = a.astype(BF16)
        w = w.astype(BF16)
    part = lax.dot_general(a, w, _NT if w_t else _NN, preferred_element_type=F32, precision=precision)

    def finish(acc):
        if has_res:
            acc = res_ref[...] + gate_ref[0] * acc
        o_ref[...] = acc.astype(o_ref.dtype)

    if nk == 1:
        finish(part)
    else:
        @pl.when(k == 0)
        def _():
            acc_ref[...] = part

        @pl.when(k > 0)
        def _():
            acc_ref[...] += part

        @pl.when(k == nk - 1)
        def _():
            finish(acc_ref[...])


def _mm(a, w, w_idx=(), *, out_dtype=F32, res=None, gate=None, precision=None, w_t=False,
        tm=1024, tn=512, tk=4096, name="mm"):
    m, kdim = a.shape
    n = w.shape[-2] if w_t else w.shape[-1]
    assert (w.shape[-1] if w_t else w.shape[-2]) == kdim
    tm, tn, tk = min(tm, m), min(tn, n), min(tk, kdim)
    assert m % tm == 0 and kdim % tk == 0
    nk = kdim // tk
    lead = tuple(w_idx)
    if w_t:
        w_spec = pl.BlockSpec((None,) * len(lead) + (tn, tk), lambda i, j, k: lead + (j, k))
    else:
        w_spec = pl.BlockSpec((None,) * len(lead) + (tk, tn), lambda i, j, k: lead + (k, j))
    in_specs = [pl.BlockSpec((tm, tk), lambda i, j, k: (i, k)), w_spec]
    args = [a, w]
    has_res = res is not None
    if has_res:
        in_specs += [pl.BlockSpec((tm, tn), lambda i, j, k: (i, j)),
                     pl.BlockSpec((1, 1, tn), lambda i, j, k: (_row_group(i, tm), 0, j))]
        args += [res, gate.reshape(N_MOD_GROUPS, 1, n)]
    return pl.pallas_call(
        functools.partial(_mm_kernel, nk=nk, has_res=has_res, precision=precision, w_t=w_t),
        grid=(m // tm, pl.cdiv(n, tn), nk),
        in_specs=in_specs,
        out_specs=pl.BlockSpec((tm, tn), lambda i, j, k: (i, j)),
        out_shape=jax.ShapeDtypeStruct((m, n), out_dtype),
        scratch_shapes=[pltpu.VMEM((tm, tn), F32)],
        compiler_params=_cparams("parallel", "parallel", "arbitrary"),
        name=name,
    )(*args)


def _dot_nt(a, b, precision=None):
    return lax.dot_general(a, b, (((1,), (1,)), ((), ())), preferred_element_type=F32,
                           precision=precision)


def _softmax_read(scores, values, sink_col):
    m = scores[0].max(axis=-1, keepdims=True)
    for s in scores[1:]:
        m = jnp.maximum(m, s.max(axis=-1, keepdims=True))
    if sink_col is not None:
        m = jnp.maximum(m, sink_col)
    den = jnp.exp(sink_col - m) if sink_col is not None else 0.0
    acc = None
    for s, v in zip(scores, values):
        p = jnp.exp(s - m)
        den = den + p.sum(axis=-1, keepdims=True)
        pv = jnp.dot(p.astype(BF16), v, preferred_element_type=F32)
        acc = pv if acc is None else acc + pv
    return acc / den


def _gqa_kernel(*refs, tq, has2, band):
    if has2:
        sink_ref, q1_ref, k1_ref, v1_ref, q2_ref, k2_ref, v2_ref, o_ref = refs
    else:
        sink_ref, q1_ref, k1_ref, v1_ref, o_ref = refs
    h = pl.program_id(1)
    qi = pl.program_id(2)
    scale = HEAD_DIM ** -0.5

    def stack(q_ref):
        q = q_ref[...]
        return jnp.concatenate([q[:, g * HEAD_DIM:(g + 1) * HEAD_DIM] for g in range(A_GROUP)],
                               axis=0).astype(BF16)

    s1 = _dot_nt(stack(q1_ref), k1_ref[...].astype(BF16)) * scale
    if band:
        qpos = qi * tq + lax.broadcasted_iota(jnp.int32, s1.shape, 0) % tq
        kpos = lax.broadcasted_iota(jnp.int32, s1.shape, 1)
        s1 = jnp.where(jnp.abs(kpos - qpos) <= WINDOW, s1, NEG_INF)
    scores, values = [s1], [v1_ref[...].astype(BF16)]
    if has2:
        scores.append(_dot_nt(stack(q2_ref), k2_ref[...].astype(BF16)) * scale)
        values.append(v2_ref[...].astype(BF16))
    sink_col = jnp.concatenate([jnp.full((tq, 1), sink_ref[h * A_GROUP + g], F32)
                                for g in range(A_GROUP)], axis=0)
    out = _softmax_read(scores, values, sink_col)
    for g in range(A_GROUP):
        o_ref[:, g * HEAD_DIM:(g + 1) * HEAD_DIM] = out[g * tq:(g + 1) * tq]


def _tok_spec(rows, width, row0, rows_per_batch, col0, per_q):
    assert row0 % rows == 0 and rows_per_batch % rows == 0 and col0 % width == 0
    r0, rb, c0 = row0 // rows, rows_per_batch // rows, col0 // width
    if per_q:
        return pl.BlockSpec((rows, width), lambda b, h, qi: (r0 + b * rb + qi, c0 + h))
    return pl.BlockSpec((rows, width), lambda b, h, qi: (r0 + b * rb, c0 + h))


def _gqa_attn(sink, q1, k1, v1, seg2=None, *, nb, tq_len, band, out_rows, tq=128):
    gw = A_GROUP * HEAD_DIM
    qspec = lambda t: _tok_spec(tq, gw, t[1], tq_len, t[2], True)
    kspec = lambda t, tk: _tok_spec(tk, HEAD_DIM, t[1], tk, t[2], False)
    in_specs = [pl.BlockSpec(memory_space=pltpu.SMEM), qspec(q1), kspec(k1, tq_len), kspec(v1, tq_len)]
    args = [sink, q1[0], k1[0], v1[0]]
    if seg2 is not None:
        q2, k2, v2, tk2 = seg2
        in_specs += [qspec(q2), kspec(k2, tk2), kspec(v2, tk2)]
        args += [q2[0], k2[0], v2[0]]
    return pl.pallas_call(
        functools.partial(_gqa_kernel, tq=tq, has2=seg2 is not None, band=band),
        grid=(nb, A_KV_HEADS, tq_len // tq),
        in_specs=in_specs,
        out_specs=_tok_spec(tq, gw, 0, tq_len, 0, True),
        out_shape=jax.ShapeDtypeStruct((out_rows, A_Q_W), F32),
        compiler_params=_cparams("parallel", "parallel", "arbitrary"),
        name="gqa_attn",
    )(*args)


def _mla_kernel(*refs, has2):
    if has2:
        qn_ref, qp1_ref, kn1_ref, kp1_ref, v1_ref, qp2_ref, kn2_ref, kp2_ref, v2_ref, o_ref = refs
    else:
        qn_ref, qp1_ref, kn1_ref, kp1_ref, v1_ref, o_ref = refs
    scale = (QK_NOPE + QK_ROPE) ** -0.5
    lane = lax.broadcasted_iota(jnp.int32, qp1_ref.shape, 1)
    for hh in range(2):
        cols = slice(hh * QK_NOPE, (hh + 1) * QK_NOPE)
        mine = (lane < QK_ROPE) if hh == 0 else (lane >= QK_ROPE)
        qn = qn_ref[:, cols]

        def score(qp_ref, kn_ref, kp_ref):
            q = jnp.concatenate([qn, jnp.where(mine, qp_ref[...], 0.0)], axis=1).astype(BF16)
            k = jnp.concatenate([kn_ref[:, cols], kp_ref[...]], axis=1).astype(BF16)
            return _dot_nt(q, k) * scale

        scores = [score(qp1_ref, kn1_ref, kp1_ref)]
        values = [v1_ref[:, cols].astype(BF16)]
        if has2:
            scores.append(score(qp2_ref, kn2_ref, kp2_ref))
            values.append(v2_ref[:, cols].astype(BF16))
        o_ref[:, cols] = _softmax_read(scores, values, None)


def _mla_attn(qn, qp1, kn1, kp1, v1, seg2=None, *, nb, tq_len, out_rows, tq=256):
    pw = 2 * QK_NOPE
    qspec = lambda t, w: _tok_spec(tq, w, t[1], tq_len, t[2], True)
    kspec = lambda t, w, tk: _tok_spec(tk, w, t[1], tk, t[2], False)
    kpspec = lambda t, tk: pl.BlockSpec((tk, LANES), lambda b, h, qi: (t[1] // tk + b, 0))
    in_specs = [qspec(qn, pw), qspec(qp1, LANES), kspec(kn1, pw, tq_len), kpspec(kp1, tq_len),
                kspec(v1, pw, tq_len)]
    args = [qn[0], qp1[0], kn1[0], kp1[0], v1[0]]
    if seg2 is not None:
        qp2, kn2, kp2, v2, tk2 = seg2
        in_specs += [qspec(qp2, LANES), kspec(kn2, pw, tk2), kpspec(kp2, tk2), kspec(v2, pw, tk2)]
        args += [qp2[0], kn2[0], kp2[0], v2[0]]
    return pl.pallas_call(
        functools.partial(_mla_kernel, has2=seg2 is not None),
        grid=(nb, C_HEADS // 2, tq_len // tq),
        in_specs=in_specs,
        out_specs=_tok_spec(tq, pw, 0, tq_len, 0, True),
        out_shape=jax.ShapeDtypeStruct((out_rows, C_OUT_W), F32),
        compiler_params=_cparams("parallel", "parallel", "arbitrary"),
        name="mla_attn",
    )(*args)


def _bdot(a, b, dims=_NN):
    return lax.dot_general(a.astype(BF16), b.astype(BF16), dims, preferred_element_type=F32)


def _bf16_parts(x, n):
    parts = []
    for _ in range(n - 1):
        hi = x.astype(BF16)
        parts.append(hi)
        x = x - hi.astype(F32)
    parts.append(x.astype(BF16))
    return parts


def _rwkv_kernel(r_ref, lw_ref, k_ref, v_ref, a_ref, b_ref, s0_ref, y_ref, s_ref,
                 p_scr, q_scr, rw_scr, y0_scr, *, seq, pairs, reverse, unroll):
    c = RWKV_CHUNK
    nchunk = seq // c
    gl = 2 * RWKV_PAIR
    hg = gl // D_HEAD_SIZE
    cs = hg * c
    width = pairs * RWKV_PAIR
    row = lax.broadcasted_iota(jnp.int32, (c, c), 0)
    col = lax.broadcasted_iota(jnp.int32, (c, c), 1)
    tri = ((col >= row) if reverse else (col <= row)).astype(BF16)
    last = 0 if reverse else c - 1
    head_of_lane = lax.broadcasted_iota(jnp.int32, (c, gl), 1) // D_HEAD_SIZE
    grow = lax.broadcasted_iota(jnp.int32, (cs, 2 * cs), 0)
    gcol = lax.broadcasted_iota(jnp.int32, (cs, 2 * cs), 1)
    gcol_in = jnp.where(gcol >= cs, gcol - cs, gcol)
    before = (gcol_in > grow) if reverse else (gcol_in < grow)
    upto = (gcol_in >= grow) if reverse else (gcol_in <= grow)
    eye = (lax.broadcasted_iota(jnp.int32, (cs, cs), 0) ==
           lax.broadcasted_iota(jnp.int32, (cs, cs), 1)).astype(F32)
    eye_pair = (lax.broadcasted_iota(jnp.int32, (RWKV_PAIR, RWKV_PAIR), 0) ==
                lax.broadcasted_iota(jnp.int32, (RWKV_PAIR, RWKV_PAIR), 1))
    n_double = c.bit_length() - 2

    def stacked(x):
        return jnp.concatenate([jnp.where(head_of_lane == h, x, 0.0) for h in range(hg)], axis=0)

    def local(step, carry):
        items = []
        for ui in range(unroll):
            cidx = step * unroll + ui
            rows = pl.ds(pl.multiple_of(cidx * c, c), c)
            r, lw, k = r_ref[0, rows, :], lw_ref[0, rows, :], k_ref[0, rows, :]
            v, a, b = v_ref[0, rows, :], a_ref[0, rows, :], b_ref[0, rows, :]
            cum = jnp.dot(tri, jnp.concatenate(_bf16_parts(lw, 3), axis=1), preferred_element_type=F32)
            lg = cum[:, :width] + (cum[:, width:2 * width] + cum[:, 2 * width:])
            lg_end = lg[last:last + 1]
            g_inv = jnp.exp(-lg)
            g_end = jnp.exp(lg_end - lg)
            g_chunk = jnp.exp(lg_end)
            full = (a * jnp.exp(lg - lw), r * jnp.exp(lg), b * g_inv, k * g_inv, v, b * g_end, k * g_end)
            for g in range(pairs // 2):
                ops = tuple(stacked(x[:, g * gl:(g + 1) * gl]) for x in full)
                items.append((cidx, g, g_chunk) + ops)
        n = range(len(items))
        at, rt, bt, kt, vs, bh, kh = ([it[3 + j] for it in items] for j in range(7))
        gram = [_bdot(jnp.concatenate([at[i], rt[i]], axis=0), jnp.concatenate([bt[i], kt[i]], axis=0), _NT)
                for i in n]
        g_u = [jnp.where(before, gram[i][:cs], 0.0) for i in n]
        g_y = [jnp.where(upto, gram[i][cs:], 0.0) for i in n]
        low = [g_u[i][:, :cs] for i in n]
        tinv = [eye + low[i] for i in n]
        pw = [_bdot(low[i], low[i]) for i in n]
        c0 = [_bdot(g_u[i][:, cs:], vs[i]) for i in n]
        for it in range(n_double):
            if it < n_double - 1:
                both = [_bdot(jnp.concatenate([pw[i], tinv[i]], axis=0), pw[i]) for i in n]
                pw = [both[i][:cs] for i in n]
                tinv = [tinv[i] + both[i][cs:] for i in n]
            else:
                tinv = [tinv[i] + _bdot(tinv[i], pw[i]) for i in n]
        wu = [_bdot(tinv[i], jnp.concatenate([at[i], c0[i]], axis=1)) for i in n]
        pmat = [_bdot(wu[i][:, :gl], bh[i], _TN) for i in n]
        qmat = [_bdot(jnp.concatenate([wu[i][:, gl:], vs[i]], axis=0),
                      jnp.concatenate([bh[i], kh[i]], axis=0), _TN) for i in n]
        ry = [_bdot(g_y[i], jnp.concatenate([wu[i], jnp.concatenate([jnp.zeros_like(vs[i]), vs[i]], axis=1)],
                                            axis=0)) for i in n]
        for i in n:
            cidx, g, g_chunk = items[i][:3]
            rw, y0 = rt[i] + ry[i][:, :gl], ry[i][:, gl:]
            for j in range(2):
                p = 2 * g + j
                lanes = slice(j * RWKV_PAIR, (j + 1) * RWKV_PAIR)
                head_rows = slice(2 * j * c, (2 * j + 2) * c)
                decay = g_chunk[:, p * RWKV_PAIR:(p + 1) * RWKV_PAIR]
                p_scr[cidx, p] = jnp.where(eye_pair, decay, 0.0) + pmat[i][lanes, lanes]
                q_scr[cidx, p] = qmat[i][lanes, lanes]
                rw_scr[cidx, p] = rw[head_rows, lanes]
                y0_scr[cidx, p] = y0[head_rows, lanes]
        return carry

    lax.fori_loop(0, nchunk // unroll, local, 0)
    s_ref[0] = s0_ref[0]

    def scan(ci, carry):
        cidx = (nchunk - 1 - ci) if reverse else ci
        rows = pl.ds(pl.multiple_of(cidx * c, c), c)
        s = [s_ref[0, p] for p in range(pairs)]
        ys = [_bdot(rw_scr[cidx, p], s[p], _NT) for p in range(pairs)]
        s_new = [_bdot(s[p], p_scr[cidx, p]) for p in range(pairs)]
        for p in range(pairs):
            y = ys[p] + y0_scr[cidx, p]
            y_ref[0, rows, p * RWKV_PAIR:(p + 1) * RWKV_PAIR] = y[:c] + y[c:]
            s_ref[0, p] = s_new[p] + q_scr[cidx, p]
        return carry

    lax.fori_loop(0, nchunk, scan, 0)


def _rwkv_scan(r, lw, k, v, a, b, s0, *, reverse, pairs, unroll):
    nb, seq, _ = r.shape
    assert 4 * RWKV_CHUNK == RWKV_PAIR and seq % (RWKV_CHUNK * unroll) == 0
    assert pairs % 2 == 0 and N_PAIRS % pairs == 0
    nchunk = seq // RWKV_CHUNK
    tok = pl.BlockSpec((1, seq, pairs * RWKV_PAIR), lambda bi, pi: (bi, 0, pi))
    st = pl.BlockSpec((1, pairs, RWKV_PAIR, RWKV_PAIR), lambda bi, pi: (bi, pi, 0, 0))
    pair_mat = pltpu.VMEM((nchunk, pairs, RWKV_PAIR, RWKV_PAIR), F32)
    pair_rows = pltpu.VMEM((nchunk, pairs, 2 * RWKV_CHUNK, RWKV_PAIR), F32)
    return pl.pallas_call(
        functools.partial(_rwkv_kernel, seq=seq, pairs=pairs, reverse=reverse, unroll=unroll),
        grid=(nb, N_PAIRS // pairs),
        in_specs=[tok] * 6 + [st],
        out_specs=[tok, st],
        out_shape=[jax.ShapeDtypeStruct(r.shape, F32), jax.ShapeDtypeStruct(s0.shape, F32)],
        scratch_shapes=[pair_mat, pair_mat, pair_rows, pair_rows],
        compiler_params=_cparams("parallel", "parallel"),
        name="rwkv_rev" if reverse else "rwkv_fwd",
    )(r, lw, k, v, a, b, s0)


def _moe_kernel(be_ref, nv_ref, idx_ref, w_ref, h_hbm, wg_ref, wu_ref, wd_ref, y_hbm,
                xbuf, gacc, uacc, hbuf, acc, gsem, ssem):
    i = pl.program_id(0)
    f = pl.program_id(1)
    n_used = nv_ref[0]
    cur = i % 2

    def row_in(blk, r, dst):
        tok = idx_ref[blk * MOE_ROWS + r] & MOE_TOK_MASK
        return pltpu.make_async_copy(h_hbm.at[pl.ds(tok, 1)], xbuf.at[dst, pl.ds(r, 1)], gsem.at[dst])

    def row_out(r):
        slot = idx_ref[i * MOE_ROWS + r] >> MOE_TOK_BITS
        return pltpu.make_async_copy(acc.at[pl.ds(r, 1)], y_hbm.at[pl.ds(slot, 1)], ssem)

    def gather(blk, dst):
        def body(r, carry):
            row_in(blk, r, dst).start()
            return carry
        lax.fori_loop(0, MOE_ROWS, body, 0, unroll=8)

    @pl.when((f == 0) & (i == 0))
    def _():
        n_slots = y_hbm.shape[0] - MOE_ROWS
        acc[...] = jnp.zeros_like(acc)
        tail = pltpu.make_async_copy(acc, y_hbm.at[pl.ds(n_slots, MOE_ROWS)], ssem)
        tail.start()
        tail.wait()

    @pl.when((f == 0) & (i == 0) & (n_used > 0))
    def _():
        gather(0, 0)

    @pl.when((f == 0) & (i < n_used))
    def _():
        pltpu.make_async_copy(h_hbm.at[pl.ds(0, MOE_ROWS)], xbuf.at[cur], gsem.at[cur]).wait()

        @pl.when(i + 1 < n_used)
        def _():
            gather(i + 1, 1 - cur)

    half_d = xbuf.shape[2] // 2
    half_ff = hbuf.shape[1] // 2

    def gate_up(lo):
        x = xbuf[cur, :, lo:lo + half_d].astype(BF16)
        return (jnp.dot(x, wg_ref[...].astype(BF16), preferred_element_type=F32),
                jnp.dot(x, wu_ref[...].astype(BF16), preferred_element_type=F32))

    def down(lo):
        return jnp.dot(hbuf[:, lo:lo + half_ff], wd_ref[...].astype(BF16), preferred_element_type=F32)

    @pl.when((i < n_used) & (f == 0))
    def _():
        gacc[...], uacc[...] = gate_up(0)

    @pl.when((i < n_used) & (f == 1))
    def _():
        g, u = gate_up(half_d)
        g, u = gacc[...] + g, uacc[...] + u
        hbuf[...] = ((g * jax.nn.sigmoid(g)) * u).astype(BF16)
        acc[...] = down(0)

    @pl.when((i < n_used) & (f == 2))
    def _():
        acc[...] = (acc[...] + down(half_ff)) * w_ref[...]

        def body(r, carry):
            row_out(r).start()
            return carry
        lax.fori_loop(0, MOE_ROWS, body, 0, unroll=8)
        pltpu.make_async_copy(acc, y_hbm.at[pl.ds(0, MOE_ROWS)], ssem).wait()


def _moe_experts(layer, block_e, n_used, row_idx, row_w, h, w_gate, w_up, w_down):
    n_blocks = block_e.shape[0]
    n_tok, d = h.shape
    ff = w_gate.shape[-1]
    in_half = pl.BlockSpec((None, None, d // 2, ff), lambda i, f, be, nv, ix: (layer, be[i], jnp.minimum(f, 1), 0))
    return pl.pallas_call(
        _moe_kernel,
        grid_spec=pltpu.PrefetchScalarGridSpec(
            num_scalar_prefetch=3,
            grid=(n_blocks, 3),
            in_specs=[pl.BlockSpec((MOE_ROWS, 1), lambda i, f, be, nv, ix: (i, 0)),
                      pl.BlockSpec(memory_space=pl.ANY),
                      in_half, in_half,
                      pl.BlockSpec((None, None, ff // 2, d),
                                   lambda i, f, be, nv, ix: (layer, be[i], jnp.maximum(f - 1, 0), 0))],
            out_specs=pl.BlockSpec(memory_space=pl.ANY),
            scratch_shapes=[pltpu.VMEM((2, MOE_ROWS, d), F32), pltpu.VMEM((MOE_ROWS, ff), F32),
                            pltpu.VMEM((MOE_ROWS, ff), F32), pltpu.VMEM((MOE_ROWS, ff), BF16),
                            pltpu.VMEM((MOE_ROWS, d), F32),
                            pltpu.SemaphoreType.DMA((2,)), pltpu.SemaphoreType.DMA(())]),
        out_shape=jax.ShapeDtypeStruct((n_tok * TOP_K + MOE_ROWS, d), F32),
        compiler_params=_cparams("arbitrary", "arbitrary"),
        name="moe_experts",
    )(block_e, n_used, row_idx, row_w, h, w_gate, w_up, w_down)


def _moe(layer, h_f32, w_router, b_router, w_gate, w_up, w_down):
    n_tok = h_f32.shape[0]
    logits = _mm(h_f32, w_router, precision=HIGHEST, tm=256, tn=LANES, name="router")
    logits = logits[:, :N_GROUPS + N_EXPERTS] + b_router
    gp = jax.nn.softmax(logits[:, :N_GROUPS], axis=-1)
    g_idx = jnp.argmax(gp, axis=-1)[:, None]
    p_g = jnp.max(gp, axis=-1, keepdims=True)
    el = logits[:, N_GROUPS:].reshape(-1, N_GROUPS, EXPERTS_PER_GROUP)
    el = jnp.take_along_axis(el, g_idx[:, :, None], axis=1)[:, 0]
    pe = jax.nn.softmax(el, axis=-1)
    e_first = jnp.argmax(pe, axis=-1)[:, None]
    pe_rest = jnp.where(jnp.arange(EXPERTS_PER_GROUP)[None, :] == e_first, -jnp.inf, pe)
    e_loc = jnp.concatenate([e_first, jnp.argmax(pe_rest, axis=-1)[:, None]], axis=1)
    p_e = jnp.concatenate([jnp.max(pe, axis=-1, keepdims=True), jnp.max(pe_rest, axis=-1, keepdims=True)], axis=1)
    wts = p_g * p_e / jnp.sum(p_e, axis=-1, keepdims=True)
    idx = g_idx * EXPERTS_PER_GROUP + e_loc

    n_slots = n_tok * TOP_K
    assert n_tok <= MOE_TOK_MASK and n_slots % MOE_ROWS == 0
    flat_e = idx.reshape(-1).astype(jnp.int32)
    order = jnp.argsort(flat_e).astype(jnp.int32)
    sorted_e = flat_e[order]
    counts = jnp.bincount(flat_e, length=N_EXPERTS)
    padded = (counts + MOE_ROWS - 1) // MOE_ROWS * MOE_ROWS
    pad_end = jnp.cumsum(padded)
    pad_start = pad_end - padded
    start = jnp.cumsum(counts) - counts
    dest_sorted = (pad_start[sorted_e] + jnp.arange(n_slots) - start[sorted_e]).astype(jnp.int32)
    n_blocks = n_slots // MOE_ROWS + N_EXPERTS
    n_rows = n_blocks * MOE_ROWS
    pad_idx = (n_slots + jnp.arange(n_rows, dtype=jnp.int32) % MOE_ROWS) << MOE_TOK_BITS
    tok = order // TOP_K
    slot = (order % TOP_K) * n_tok + tok
    row_idx = pad_idx.at[dest_sorted].set(tok | (slot << MOE_TOK_BITS))
    row_w = jnp.zeros((n_rows,), F32).at[dest_sorted].set(wts.reshape(-1)[order])
    block_e = jnp.minimum(jnp.searchsorted(pad_end, jnp.arange(n_blocks) * MOE_ROWS, side='right'),
                          N_EXPERTS - 1).astype(jnp.int32)
    n_used = (pad_end[-1] // MOE_ROWS).astype(jnp.int32).reshape(1)
    return _moe_experts(layer, block_e, n_used, row_idx, row_w.reshape(n_rows, 1), h_f32, w_gate, w_up, w_down)


def _axial_rope_tables(n_tok, rot_dim):
    t = jnp.arange(n_tok)
    rows = (t // GRID_W).astype(F32)
    cols = (t % GRID_W).astype(F32)
    axis_dim = rot_dim // 2
    inv = ROPE_BASE ** (-jnp.arange(0, axis_dim, 2, dtype=F32) / axis_dim)
    ang_r = rows[:, None] * inv[None, :]
    ang_c = cols[:, None] * inv[None, :]
    return jnp.cos(ang_r), jnp.sin(ang_r), jnp.cos(ang_c), jnp.sin(ang_c)


def _rope_rotate(x, cos, sin):
    half = x.shape[-1] // 2
    c, s = cos[None, :, None, :], sin[None, :, None, :]
    x1, x2 = x[..., :half], x[..., half:]
    return jnp.concatenate([x1 * c - x2 * s, x2 * c + x1 * s], axis=-1)


def _apply_axial_rope(x, tables):
    cos_r, sin_r, cos_c, sin_c = tables
    h = x.shape[-1] // 2
    return jnp.concatenate([_rope_rotate(x[..., :h], cos_r, sin_r),
                            _rope_rotate(x[..., h:], cos_c, sin_c)], axis=-1)


def _short_conv(bg, cg, u, conv_w, nb):
    z = (cg * u).reshape(nb, -1, B_WIDTH)
    zp = jnp.pad(z, ((0, 0), (1, 1), (0, 0)))
    conv = conv_w[0] * zp[:, :-2] + conv_w[1] * zp[:, 1:-1] + conv_w[2] * zp[:, 2:]
    return bg * conv.reshape(-1, B_WIDTH)


def _even_mixer(h, i, cache_k, cache_v, w_in, w_out, sink, conv_w, x, gate):
    p = _mm(h, w_in, (i,), name="even_in")
    o_k, o_v, o_b = A_Q_W, A_Q_W + A_KV_W, A_Q_W + 2 * A_KV_W
    att_c = _gqa_attn(sink[i], (p, 0, 0), (p, 0, o_k), (p, 0, o_v),
                      nb=BATCH, tq_len=SEQ, band=False, out_rows=N_CTX_TOK)
    pl_ = p[N_CTX_TOK:]
    tabs = _axial_rope_tables(DEC_SEQ, HEAD_DIM)
    q_rot = _apply_axial_rope(pl_[:, :A_Q_W].reshape(DEC_BATCH, DEC_SEQ, A_HEADS, HEAD_DIM), tabs)
    k_rot = _apply_axial_rope(pl_[:, o_k:o_v].reshape(DEC_BATCH, DEC_SEQ, A_KV_HEADS, HEAD_DIM), tabs)
    ck = cache_k[:, i].reshape(DEC_BATCH * PAST_LEN, A_KV_W)
    cv = cache_v[:, i].reshape(DEC_BATCH * PAST_LEN, A_KV_W)
    att_l = _gqa_attn(sink[i], (q_rot.reshape(N_LAT_TOK, A_Q_W), 0, 0),
                      (k_rot.reshape(N_LAT_TOK, A_KV_W), 0, 0), (p, N_CTX_TOK, o_v),
                      ((p, N_CTX_TOK, 0), (ck, 0, 0), (cv, 0, 0), PAST_LEN),
                      nb=DEC_BATCH, tq_len=DEC_SEQ, band=True, out_rows=N_LAT_TOK)
    bg, cg, u = (p[:, o_b + j * B_WIDTH:o_b + (j + 1) * B_WIDTH] for j in range(3))
    conv = jnp.concatenate([_short_conv(bg[:N_CTX_TOK], cg[:N_CTX_TOK], u[:N_CTX_TOK], conv_w[i], BATCH),
                            _short_conv(bg[N_CTX_TOK:], cg[N_CTX_TOK:], u[N_CTX_TOK:], conv_w[i], DEC_BATCH)],
                           axis=0)
    cat = jnp.concatenate([jnp.concatenate([att_c, att_l], axis=0), conv], axis=-1).astype(BF16)
    x = _mm(cat, w_out, (i,), res=x, gate=gate, name="even_out")
    new_k = p[:N_CTX_TOK, o_k:o_v].reshape(BATCH, SEQ, A_KV_HEADS, HEAD_DIM)
    new_v = p[:N_CTX_TOK, o_v:o_b].reshape(BATCH, SEQ, A_KV_HEADS, HEAD_DIM)
    return x, new_k, new_v


def _rms(x, g):
    return x * lax.rsqrt(jnp.mean(x * x, axis=-1, keepdims=True) + NORM_EPS) * g


def _rwkv_group(rw, g_lo, s0, i, rp, nb, pairs):
    seq = rw.shape[0] // nb
    rw3 = rw.reshape(nb, seq, SHIFT_W)
    outs, finals = [], []
    for d in range(2):
        if d == 0:
            nbr = jnp.pad(rw3, ((0, 0), (1, 0), (0, 0)))[:, :-1]
        else:
            nbr = jnp.pad(rw3, ((0, 0), (0, 1), (0, 0)))[:, 1:]
        z = (rw3 + rp['mu'][i, d] * (nbr - rw3)).reshape(nb * seq, SHIFT_W)
        r, k, v = (z[:, j * D_WIDTH:(j + 1) * D_WIDTH] for j in range(3))
        w_lo = z[:, 3 * D_WIDTH:3 * D_WIDTH + DECAY_LORA]
        a_lo = z[:, 3 * D_WIDTH + DECAY_LORA:]
        w_raw = rp['w0'][i, d] + _mm(jnp.tanh(w_lo), rp['w_up'], (i, d), name="rwkv_decay")
        lw = -jnp.exp(-jax.nn.softplus(-w_raw) - 0.5)
        a = jax.nn.sigmoid(rp['a0'][i, d] + _mm(a_lo, rp['a_up'], (i, d), name="rwkv_aaa"))
        heads = lambda t: t.reshape(nb * seq, D_HEADS, D_HEAD_SIZE)
        kk = heads(k * rp['k_k'][i, d])
        kk = kk / jnp.maximum(jnp.sqrt(jnp.sum(kk * kk, axis=-1, keepdims=True)), 1e-12)
        kk = kk.reshape(nb * seq, D_WIDTH)
        k = k * (1 + (a - 1) * rp['k_a'][i, d])
        to3 = lambda t: t.reshape(nb, seq, D_WIDTH)
        sd = s0[:, d].reshape(nb, N_PAIRS, 2, D_HEAD_SIZE, D_HEAD_SIZE)
        eye2 = jnp.eye(2, dtype=F32)
        s_bd = jnp.einsum('bphij,hg->bphigj', sd, eye2).reshape(nb, N_PAIRS, RWKV_PAIR, RWKV_PAIR)
        y, s_fin = _rwkv_scan(to3(r), to3(lw), to3(k), to3(v), to3(-kk), to3(kk * a), s_bd,
                              reverse=(d == 1), pairs=pairs, unroll=16 // pairs)
        s_fin = s_fin.reshape(nb, N_PAIRS, 2, D_HEAD_SIZE, 2, D_HEAD_SIZE)
        s_fin = jnp.stack([s_fin[:, :, 0, :, 0], s_fin[:, :, 1, :, 1]], axis=2)
        finals.append(s_fin.reshape(nb, D_HEADS, D_HEAD_SIZE, D_HEAD_SIZE))
        yh = heads(y.reshape(nb * seq, D_WIDTH))
        mu = jnp.mean(yh, axis=-1, keepdims=True)
        var = jnp.mean(jnp.square(yh - mu), axis=-1, keepdims=True)
        yn = ((yh - mu) * lax.rsqrt(var + GN_EPS)).reshape(nb * seq, D_WIDTH) * rp['ln_w'][i] + rp['ln_b'][i]
        bonus = jnp.sum(heads(r) * heads(k) * rp['r_k'][i, d], axis=-1, keepdims=True) * heads(v)
        outs.append(yn + bonus.reshape(nb * seq, D_WIDTH))
    g = _mm(jax.nn.sigmoid(g_lo), rp['g_up'], (i,), name="rwkv_gate")
    return (outs[0] + outs[1]) * g, jnp.stack(finals, axis=1)


def _odd_mixer(h, i, cache_ckv, cache_kpe, state, w_in_t, w_out, mp, rp, x, gate):
    p = _mm(h, w_in_t, (i,), w_t=True, name="odd_in")
    o1 = Q_LORA
    o2 = o1 + KV_LORA
    o3 = o2 + QK_ROPE
    o4 = o3 + SHIFT_W
    q_down, kv_down, k_pe, rw, g_lo = p[:, :o1], p[:, o1:o2], p[:, o2:o3], p[:, o3:o4], p[:, o4:]
    w_uq = mp['w_uq'][i].reshape(Q_LORA, C_HEADS, QK_NOPE + QK_ROPE)
    w_uq = jnp.concatenate([w_uq[:, :, :QK_NOPE].reshape(Q_LORA, -1), w_uq[:, :, QK_NOPE:].reshape(Q_LORA, -1)],
                           axis=1)
    w_ukv = mp['w_ukv'][i].reshape(KV_LORA, C_HEADS, QK_NOPE + V_DIM)
    w_ukv = jnp.concatenate([w_ukv[:, :, :QK_NOPE].reshape(KV_LORA, -1), w_ukv[:, :, QK_NOPE:].reshape(KV_LORA, -1)],
                            axis=1)
    q = _mm(_rms(q_down, mp['q_norm'][i]), w_uq, name="mla_uq")
    c_kv = _rms(kv_down, mp['kv_norm'][i])
    kv = _mm(c_kv, w_ukv, name="mla_ukv")
    kv_ctx = _mm(cache_ckv[:, i].reshape(DEC_BATCH * PAST_LEN, KV_LORA), w_ukv, name="mla_ukv_cache")
    n_nope = C_HEADS * QK_NOPE
    dup = lambda t: jnp.concatenate([t, t], axis=-1)
    kp = dup(k_pe)
    att_c = _mla_attn((q, 0, 0), (q, 0, n_nope), (kv, 0, 0), (kp, 0, 0), (kv, 0, n_nope),
                      nb=BATCH, tq_len=SEQ, out_rows=N_CTX_TOK)
    tabs = _axial_rope_tables(DEC_SEQ, QK_ROPE)
    qpe_rot = _apply_axial_rope(q[N_CTX_TOK:, n_nope:].reshape(DEC_BATCH, DEC_SEQ, C_HEADS, QK_ROPE), tabs)
    kpe_rot = _apply_axial_rope(k_pe[N_CTX_TOK:].reshape(DEC_BATCH, DEC_SEQ, 1, QK_ROPE), tabs)
    kp_ctx = dup(cache_kpe[:, i].reshape(DEC_BATCH * PAST_LEN, QK_ROPE))
    att_l = _mla_attn((q, N_CTX_TOK, 0), (qpe_rot.reshape(N_LAT_TOK, -1), 0, 0), (kv, N_CTX_TOK, 0),
                      (dup(kpe_rot.reshape(N_LAT_TOK, QK_ROPE)), 0, 0), (kv, N_CTX_TOK, n_nope),
                      ((q, N_CTX_TOK, n_nope), (kv_ctx, 0, 0), (kp_ctx, 0, 0), (kv_ctx, 0, n_nope), PAST_LEN),
                      nb=DEC_BATCH, tq_len=DEC_SEQ, out_rows=N_LAT_TOK)
    s_zero = jnp.zeros((BATCH, 2, D_HEADS, D_HEAD_SIZE, D_HEAD_SIZE), F32)
    rw_c, s_c = _rwkv_group(rw[:N_CTX_TOK], g_lo[:N_CTX_TOK], s_zero, i, rp, BATCH, 4)
    rw_l, _ = _rwkv_group(rw[N_CTX_TOK:], g_lo[N_CTX_TOK:], state[:, i].astype(F32), i, rp, DEC_BATCH, 2)
    cat = jnp.concatenate([jnp.concatenate([att_c, att_l], axis=0),
                           jnp.concatenate([rw_c, rw_l], axis=0)], axis=-1).astype(BF16)
    x = _mm(cat, w_out, (i,), res=x, gate=gate, name="odd_out")
    new_ckv = c_kv[:N_CTX_TOK].reshape(BATCH, SEQ, KV_LORA)
    new_kpe = k_pe[:N_CTX_TOK].reshape(BATCH, SEQ, QK_ROPE)
    return x, new_ckv, new_kpe, s_c


def kernel(x_prompt, x_sample, c, c_ctx, cache_attn_k, cache_attn_v, cache_mla_ckv, cache_mla_kpe,
           state_rwkv, w_mod, b_mod, norm_mix, norm_ffn, norm_final, even_w_in, even_w_out, attn_sink,
           conv_w, odd_w_in, odd_w_out, mla_q_norm, mla_kv_norm, mla_w_uq, mla_w_ukv, rwkv_mu, rwkv_w0,
           rwkv_w_up, rwkv_a0, rwkv_a_up, rwkv_k_k, rwkv_k_a, rwkv_r_k, rwkv_g_up, rwkv_ln_w, rwkv_ln_b,
           router_group_w, router_group_b, router_expert_w, router_expert_b, expert_w_gate, expert_w_up,
           expert_w_down):
    d = D_MODEL
    x = jnp.concatenate([x_prompt.reshape(N_CTX_TOK, d), x_sample.reshape(N_LAT_TOK, d)], axis=0)
    cvec = jax.nn.silu(jnp.concatenate([c_ctx[None], c], axis=0))
    cvec = jnp.pad(cvec, ((0, 8 - N_MOD_GROUPS), (0, 0)))
    mp = dict(q_norm=mla_q_norm, kv_norm=mla_kv_norm, w_uq=mla_w_uq, w_ukv=mla_w_ukv)
    rp = dict(mu=rwkv_mu, w0=rwkv_w0, w_up=rwkv_w_up, a0=rwkv_a0, a_up=rwkv_a_up, k_k=rwkv_k_k,
              k_a=rwkv_k_a, r_k=rwkv_r_k, g_up=rwkv_g_up, ln_w=rwkv_ln_w, ln_b=rwkv_ln_b)
    odd_w_in_t = jnp.swapaxes(odd_w_in, 1, 2)
    new_k, new_v, new_ckv, new_kpe, new_s = [], [], [], [], []
    ffn = None
    for l in range(DEPTH):
        mod = _mm(cvec, w_mod, (l,), tn=1024, tk=2048, name="modulation")[:N_MOD_GROUPS] + b_mod[l]
        shift_mix, scale_mix, gate_mix, shift_ffn, scale_ffn, gate_ffn = jnp.split(mod, 6, axis=-1)
        if ffn is None:
            (h,) = _normmod(x, norm_mix[l], shift_mix, scale_mix, (BF16,))
        else:
            x, h = _normmod(x, norm_mix[l], shift_mix, scale_mix, (BF16,), res=ffn, emit_x=True)
        i = l // 2
        if l % 2 == 0:
            x, k_c, v_c = _even_mixer(h, i, cache_attn_k, cache_attn_v, even_w_in, even_w_out, attn_sink,
                                      conv_w, x, gate_mix)
            new_k.append(k_c)
            new_v.append(v_c)
        else:
            x, ckv_c, kpe_c, s_c = _odd_mixer(h, i, cache_mla_ckv, cache_mla_kpe, state_rwkv, odd_w_in_t,
                                              odd_w_out, mp, rp, x, gate_mix)
            new_ckv.append(ckv_c)
            new_kpe.append(kpe_c)
            new_s.append(s_c)
        (h_ffn,) = _normmod(x, norm_ffn[l], shift_ffn, scale_ffn, (F32,))
        w_router = jnp.pad(jnp.concatenate([router_group_w[l], router_expert_w[l]], axis=1),
                           ((0, 0), (0, LANES - N_GROUPS - N_EXPERTS)))
        b_router = jnp.concatenate([router_group_b[l], router_expert_b[l]])
        ffn = (_moe(l, h_ffn, w_router, b_router, expert_w_gate, expert_w_up, expert_w_down), gate_ffn)
    zeros = jnp.zeros((N_MOD_GROUPS, d), F32)
    (y_all,) = _normmod(x, norm_final, zeros, zeros, (F32,), res=ffn)
    return (y_all[:N_CTX_TOK].reshape(BATCH, SEQ, d), y_all[N_CTX_TOK:].reshape(DEC_BATCH, DEC_SEQ, d),
            jnp.stack(new_k, axis=1), jnp.stack(new_v, axis=1), jnp.stack(new_ckv, axis=1),
            jnp.stack(new_kpe, axis=1), jnp.stack(new_s, axis=1))
```

```python
import functools

import jax
import jax.numpy as jnp
from jax import lax
from jax.experimental import pallas as pl
from jax.experimental.pallas import tpu as pltpu

F32 = jnp.float32
BF16 = jnp.bfloat16
HIGHEST = lax.Precision.HIGHEST

D_MODEL = 4096
BATCH = 16
SEQ = 256
DEPTH = 4
DEC_BATCH = 2
DEC_SEQ = 1024
PAST_LEN = 512
GRID_W = 64
WINDOW = 128
ROPE_BASE = 10000.0
NORM_EPS = 1e-6
NEG_INF = -1e30

HEAD_DIM = 128
A_HEADS = 16
A_KV_HEADS = 4
A_GROUP = A_HEADS // A_KV_HEADS
A_Q_W = A_HEADS * HEAD_DIM
A_KV_W = A_KV_HEADS * HEAD_DIM
B_WIDTH = D_MODEL // 2
C_HEADS = 16
Q_LORA = D_MODEL // 4
KV_LORA = D_MODEL // 8
QK_NOPE = 128
QK_ROPE = 64
V_DIM = 128
C_OUT_W = C_HEADS * V_DIM
D_WIDTH = D_MODEL // 2
D_HEAD_SIZE = 64
D_HEADS = D_WIDTH // D_HEAD_SIZE
DECAY_LORA = 64
AAA_LORA = 64
GATE_LORA = 256
GN_EPS = 64e-5
SHIFT_W = 3 * D_WIDTH + DECAY_LORA + AAA_LORA
N_GROUPS = 8
EXPERTS_PER_GROUP = 8
N_EXPERTS = N_GROUPS * EXPERTS_PER_GROUP
TOP_K = 2
EXPERT_FF = 512

N_CTX_TOK = BATCH * SEQ
N_LAT_TOK = DEC_BATCH * DEC_SEQ
N_TOK = N_CTX_TOK + N_LAT_TOK
N_MOD_GROUPS = 1 + DEC_BATCH

LANES = 128
VMEM_LIMIT_BYTES = 56 * 1024 * 1024
MOE_ROWS = 256
MOE_DMA_GROUP = 8
MOE_TOK_BITS = 13
MOE_TOK_MASK = (1 << MOE_TOK_BITS) - 1
RWKV_CHUNK = 32
RWKV_PAIR = 2 * D_HEAD_SIZE
N_PAIRS = D_WIDTH // RWKV_PAIR


_NN = (((1,), (0,)), ((), ()))
_NT = (((1,), (1,)), ((), ()))
_TN = (((0,), (0,)), ((), ()))


def _cparams(*sem):
    return pltpu.CompilerParams(dimension_semantics=sem, vmem_limit_bytes=VMEM_LIMIT_BYTES)


def _row_group(i, tm):
    nc = N_CTX_TOK // tm
    nl = DEC_SEQ // tm
    return jnp.where(i < nc, 0, 1 + (i - nc) // nl)


def _normmod_kernel(*refs, has_res, emit_x):
    if has_res:
        x_ref, ya_ref, yb_ref, gate_ref, g_ref, shift_ref, scale_ref, *out_refs = refs
        x = x_ref[...] + gate_ref[0] * (ya_ref[...] + yb_ref[...])
    else:
        x_ref, g_ref, shift_ref, scale_ref, *out_refs = refs
        x = x_ref[...]
    if emit_x:
        out_refs[0][...] = x
        out_refs = out_refs[1:]
    y = x * lax.rsqrt(jnp.mean(x * x, axis=-1, keepdims=True) + NORM_EPS)
    y = y * g_ref[...]
    y = y * (1 + scale_ref[0]) + shift_ref[0]
    for o in out_refs:
        o[...] = y.astype(o.dtype)


def _normmod(x, g, shift, scale, out_dtypes, res=None, emit_x=False, tm=128):
    n, d = x.shape
    grp = lambda i: (_row_group(i, tm), 0, 0)
    row = pl.BlockSpec((tm, d), lambda i: (i, 0))
    vec = pl.BlockSpec((1, 1, d), grp)
    in_specs, args = [row], [x]
    if res is not None:
        y2, gate = res
        in_specs += [row, pl.BlockSpec((tm, d), lambda i: (n // tm + i, 0)), vec]
        args += [y2, y2, gate.reshape(-1, 1, d)]
    in_specs += [pl.BlockSpec((1, d), lambda i: (0, 0)), vec, vec]
    args += [g.reshape(1, d), shift.reshape(-1, 1, d), scale.reshape(-1, 1, d)]
    out_dtypes = ((F32,) if emit_x else ()) + tuple(out_dtypes)
    return pl.pallas_call(
        functools.partial(_normmod_kernel, has_res=res is not None, emit_x=emit_x),
        grid=(n // tm,),
        in_specs=in_specs,
        out_specs=[row for _ in out_dtypes],
        out_shape=[jax.ShapeDtypeStruct((n, d), dt) for dt in out_dtypes],
        compiler_params=_cparams("parallel"),
        name="normmod",
    )(*args)


def _mm_kernel(*refs, nk, has_res, precision, w_t):
    if has_res:
        a_ref, w_ref, res_ref, gate_ref, o_ref, acc_ref = refs
    else:
        a_ref, w_ref, o_ref, acc_ref = refs
    k = pl.program_id(2)
    a = a_ref[...]
    w = w_ref[...]
    if precision is None:
        a = a.astype(BF16)
        w = w.astype(BF16)
    part = lax.dot_general(a, w, _NT if w_t else _NN, preferred_element_type=F32, precision=precision)

    def finish(acc):
        if has_res:
            acc = res_ref[...] + gate_ref[0] * acc
        o_ref[...] = acc.astype(o_ref.dtype)

    if nk == 1:
        finish(part)
    else:
        @pl.when(k == 0)
        def _():
            acc_ref[...] = part

        @pl.when(k > 0)
        def _():
            acc_ref[...] += part

        @pl.when(k == nk - 1)
        def _():
            finish(acc_ref[...])


def _mm(a, w, w_idx=(), *, out_dtype=F32, res=None, gate=None, precision=None, w_t=False,
        tm=1024, tn=512, tk=4096, name="mm"):
    m, kdim = a.shape
    n = w.shape[-2] if w_t else w.shape[-1]
    assert (w.shape[-1] if w_t else w.shape[-2]) == kdim
    tm, tn, tk = min(tm, m), min(tn, n), min(tk, kdim)
    assert m % tm == 0 and kdim % tk == 0
    nk = kdim // tk
    lead = tuple(w_idx)
    if w_t:
        w_spec = pl.BlockSpec((None,) * len(lead) + (tn, tk), lambda i, j, k: lead + (j, k))
    else:
        w_spec = pl.BlockSpec((None,) * len(lead) + (tk, tn), lambda i, j, k: lead + (k, j))
    in_specs = [pl.BlockSpec((tm, tk), lambda i, j, k: (i, k)), w_spec]
    args = [a, w]
    has_res = res is not None
    if has_res:
        in_specs += [pl.BlockSpec((tm, tn), lambda i, j, k: (i, j)),
                     pl.BlockSpec((1, 1, tn), lambda i, j, k: (_row_group(i, tm), 0, j))]
        args += [res, gate.reshape(N_MOD_GROUPS, 1, n)]
    return pl.pallas_call(
        functools.partial(_mm_kernel, nk=nk, has_res=has_res, precision=precision, w_t=w_t),
        grid=(m // tm, pl.cdiv(n, tn), nk),
        in_specs=in_specs,
        out_specs=pl.BlockSpec((tm, tn), lambda i, j, k: (i, j)),
        out_shape=jax.ShapeDtypeStruct((m, n), out_dtype),
        scratch_shapes=[pltpu.VMEM((tm, tn), F32)],
        compiler_params=_cparams("parallel", "parallel", "arbitrary"),
        name=name,
    )(*args)


def _dot_nt(a, b, precision=None):
    return lax.dot_general(a, b, (((1,), (1,)), ((), ())), preferred_element_type=F32,
                           precision=precision)


def _softmax_read(scores, values, sink_col):
    m = scores[0].max(axis=-1, keepdims=True)
    for s in scores[1:]:
        m = jnp.maximum(m, s.max(axis=-1, keepdims=True))
    if sink_col is not None:
        m = jnp.maximum(m, sink_col)
    den = jnp.exp(sink_col - m) if sink_col is not None else 0.0
    acc = None
    for s, v in zip(scores, values):
        p = jnp.exp(s - m)
        den = den + p.sum(axis=-1, keepdims=True)
        pv = jnp.dot(p.astype(BF16), v, preferred_element_type=F32)
        acc = pv if acc is None else acc + pv
    return acc / den


def _gqa_kernel(*refs, tq, has2, band):
    if has2:
        sink_ref, q1_ref, k1_ref, v1_ref, q2_ref, k2_ref, v2_ref, o_ref = refs
    else:
        sink_ref, q1_ref, k1_ref, v1_ref, o_ref = refs
    h = pl.program_id(1)
    qi = pl.program_id(2)
    scale = HEAD_DIM ** -0.5

    def stack(q_ref):
        q = q_ref[...]
        return jnp.concatenate([q[:, g * HEAD_DIM:(g + 1) * HEAD_DIM] for g in range(A_GROUP)],
                               axis=0).astype(BF16)

    s1 = _dot_nt(stack(q1_ref), k1_ref[...].astype(BF16)) * scale
    if band:
        qpos = qi * tq + lax.broadcasted_iota(jnp.int32, s1.shape, 0) % tq
        kpos = lax.broadcasted_iota(jnp.int32, s1.shape, 1)
        s1 = jnp.where(jnp.abs(kpos - qpos) <= WINDOW, s1, NEG_INF)
    scores, values = [s1], [v1_ref[...].astype(BF16)]
    if has2:
        scores.append(_dot_nt(stack(q2_ref), k2_ref[...].astype(BF16)) * scale)
        values.append(v2_ref[...].astype(BF16))
    sink_col = jnp.concatenate([jnp.full((tq, 1), sink_ref[h * A_GROUP + g], F32)
                                for g in range(A_GROUP)], axis=0)
    out = _softmax_read(scores, values, sink_col)
    for g in range(A_GROUP):
        o_ref[:, g * HEAD_DIM:(g + 1) * HEAD_DIM] = out[g * tq:(g + 1) * tq].astype(o_ref.dtype)


def _tok_spec(rows, width, row0, rows_per_batch, col0, per_q):
    assert row0 % rows == 0 and rows_per_batch % rows == 0 and col0 % width == 0
    r0, rb, c0 = row0 // rows, rows_per_batch // rows, col0 // width
    if per_q:
        return pl.BlockSpec((rows, width), lambda b, h, qi: (r0 + b * rb + qi, c0 + h))
    return pl.BlockSpec((rows, width), lambda b, h, qi: (r0 + b * rb, c0 + h))


def _gqa_attn(sink, q1, k1, v1, seg2=None, *, nb, tq_len, band, out_rows, tq=128):
    gw = A_GROUP * HEAD_DIM
    qspec = lambda t: _tok_spec(tq, gw, t[1], tq_len, t[2], True)
    kspec = lambda t, tk: _tok_spec(tk, HEAD_DIM, t[1], tk, t[2], False)
    in_specs = [pl.BlockSpec(memory_space=pltpu.SMEM), qspec(q1), kspec(k1, tq_len), kspec(v1, tq_len)]
    args = [sink, q1[0], k1[0], v1[0]]
    if seg2 is not None:
        q2, k2, v2, tk2 = seg2
        in_specs += [qspec(q2), kspec(k2, tk2), kspec(v2, tk2)]
        args += [q2[0], k2[0], v2[0]]
    return pl.pallas_call(
        functools.partial(_gqa_kernel, tq=tq, has2=seg2 is not None, band=band),
        grid=(nb, A_KV_HEADS, tq_len // tq),
        in_specs=in_specs,
        out_specs=_tok_spec(tq, gw, 0, tq_len, 0, True),
        out_shape=jax.ShapeDtypeStruct((out_rows, A_Q_W), BF16),
        compiler_params=_cparams("parallel", "parallel", "arbitrary"),
        name="gqa_attn",
    )(*args)


def _mla_kernel(*refs, has2):
    if has2:
        qn_ref, qp1_ref, kn1_ref, kp1_ref, v1_ref, qp2_ref, kn2_ref, kp2_ref, v2_ref, o_ref = refs
    else:
        qn_ref, qp1_ref, kn1_ref, kp1_ref, v1_ref, o_ref = refs
    scale = (QK_NOPE + QK_ROPE) ** -0.5
    lane = lax.broadcasted_iota(jnp.int32, qp1_ref.shape, 1)
    for hh in range(2):
        cols = slice(hh * QK_NOPE, (hh + 1) * QK_NOPE)
        mine = (lane < QK_ROPE) if hh == 0 else (lane >= QK_ROPE)
        qn = qn_ref[:, cols]

        def score(qp_ref, kn_ref, kp_ref):
            q = jnp.concatenate([qn, jnp.where(mine, qp_ref[...], 0.0)], axis=1).astype(BF16)
            k = jnp.concatenate([kn_ref[:, cols], kp_ref[...]], axis=1).astype(BF16)
            return _dot_nt(q, k) * scale

        scores = [score(qp1_ref, kn1_ref, kp1_ref)]
        values = [v1_ref[:, cols].astype(BF16)]
        if has2:
            scores.append(score(qp2_ref, kn2_ref, kp2_ref))
            values.append(v2_ref[:, cols].astype(BF16))
        o_ref[:, cols] = _softmax_read(scores, values, None).astype(o_ref.dtype)


def _mla_attn(qn, qp1, kn1, kp1, v1, seg2=None, *, nb, tq_len, out_rows, tq=256):
    pw = 2 * QK_NOPE
    qspec = lambda t, w: _tok_spec(tq, w, t[1], tq_len, t[2], True)
    kspec = lambda t, w, tk: _tok_spec(tk, w, t[1], tk, t[2], False)
    kpspec = lambda t, tk: pl.BlockSpec((tk, LANES), lambda b, h, qi: (t[1] // tk + b, 0))
    in_specs = [qspec(qn, pw), qspec(qp1, LANES), kspec(kn1, pw, tq_len), kpspec(kp1, tq_len),
                kspec(v1, pw, tq_len)]
    args = [qn[0], qp1[0], kn1[0], kp1[0], v1[0]]
    if seg2 is not None:
        qp2, kn2, kp2, v2, tk2 = seg2
        in_specs += [qspec(qp2, LANES), kspec(kn2, pw, tk2), kpspec(kp2, tk2), kspec(v2, pw, tk2)]
        args += [qp2[0], kn2[0], kp2[0], v2[0]]
    return pl.pallas_call(
        functools.partial(_mla_kernel, has2=seg2 is not None),
        grid=(nb, C_HEADS // 2, tq_len // tq),
        in_specs=in_specs,
        out_specs=_tok_spec(tq, pw, 0, tq_len, 0, True),
        out_shape=jax.ShapeDtypeStruct((out_rows, C_OUT_W), BF16),
        compiler_params=_cparams("parallel", "parallel", "arbitrary"),
        name="mla_attn",
    )(*args)


def _bdot(a, b, dims=_NN):
    return lax.dot_general(a.astype(BF16), b.astype(BF16), dims, preferred_element_type=F32)


def _bf16_parts(x, n):
    parts = []
    for _ in range(n - 1):
        hi = x.astype(BF16)
        parts.append(hi)
        x = x - hi.astype(F32)
    parts.append(x.astype(BF16))
    return parts


def _rwkv_kernel(r_ref, lw_ref, k_ref, v_ref, a_ref, b_ref, s0_ref, y_ref, s_ref,
                 p_scr, q_scr, rw_scr, y0_scr, *, seq, pairs, reverse, unroll):
    c = RWKV_CHUNK
    nchunk = seq // c
    gl = 2 * RWKV_PAIR
    hg = gl // D_HEAD_SIZE
    cs = hg * c
    width = pairs * RWKV_PAIR
    row = lax.broadcasted_iota(jnp.int32, (c, c), 0)
    col = lax.broadcasted_iota(jnp.int32, (c, c), 1)
    tri = ((col >= row) if reverse else (col <= row)).astype(BF16)
    last = 0 if reverse else c - 1
    head_of_lane = lax.broadcasted_iota(jnp.int32, (c, gl), 1) // D_HEAD_SIZE
    grow = lax.broadcasted_iota(jnp.int32, (cs, 2 * cs), 0)
    gcol = lax.broadcasted_iota(jnp.int32, (cs, 2 * cs), 1)
    gcol_in = jnp.where(gcol >= cs, gcol - cs, gcol)
    before = (gcol_in > grow) if reverse else (gcol_in < grow)
    upto = (gcol_in >= grow) if reverse else (gcol_in <= grow)
    eye = (lax.broadcasted_iota(jnp.int32, (cs, cs), 0) ==
           lax.broadcasted_iota(jnp.int32, (cs, cs), 1)).astype(F32)
    eye_pair = (lax.broadcasted_iota(jnp.int32, (RWKV_PAIR, RWKV_PAIR), 0) ==
                lax.broadcasted_iota(jnp.int32, (RWKV_PAIR, RWKV_PAIR), 1))
    n_double = c.bit_length() - 2

    def stacked(x):
        return jnp.concatenate([jnp.where(head_of_lane == h, x, 0.0) for h in range(hg)], axis=0)

    def local(step, carry):
        items = []
        for ui in range(unroll):
            cidx = step * unroll + ui
            rows = pl.ds(pl.multiple_of(cidx * c, c), c)
            r, lw, k = r_ref[0, rows, :], lw_ref[0, rows, :], k_ref[0, rows, :]
            v, a, b = v_ref[0, rows, :], a_ref[0, rows, :], b_ref[0, rows, :]
            cum = jnp.dot(tri, jnp.concatenate(_bf16_parts(lw, 3), axis=1), preferred_element_type=F32)
            lg = cum[:, :width] + (cum[:, width:2 * width] + cum[:, 2 * width:])
            lg_end = lg[last:last + 1]
            g_inv = jnp.exp(-lg)
            g_end = jnp.exp(lg_end - lg)
            g_chunk = jnp.exp(lg_end)
            full = (a * jnp.exp(lg - lw), r * jnp.exp(lg), b * g_inv, k * g_inv, v, b * g_end, k * g_end)
            for g in range(pairs // 2):
                ops = tuple(stacked(x[:, g * gl:(g + 1) * gl]) for x in full)
                items.append((cidx, g, g_chunk) + ops)
        n = range(len(items))
        at, rt, bt, kt, vs, bh, kh = ([it[3 + j] for it in items] for j in range(7))
        gram = [_bdot(jnp.concatenate([at[i], rt[i]], axis=0), jnp.concatenate([bt[i], kt[i]], axis=0), _NT)
                for i in n]
        g_u = [jnp.where(before, gram[i][:cs], 0.0) for i in n]
        g_y = [jnp.where(upto, gram[i][cs:], 0.0) for i in n]
        low = [g_u[i][:, :cs] for i in n]
        tinv = [eye + low[i] for i in n]
        pw = [_bdot(low[i], low[i]) for i in n]
        c0 = [_bdot(g_u[i][:, cs:], vs[i]) for i in n]
        for it in range(n_double):
            if it < n_double - 1:
                both = [_bdot(jnp.concatenate([pw[i], tinv[i]], axis=0), pw[i]) for i in n]
                pw = [both[i][:cs] for i in n]
                tinv = [tinv[i] + both[i][cs:] for i in n]
            else:
                tinv = [tinv[i] + _bdot(tinv[i], pw[i]) for i in n]
        wu = [_bdot(tinv[i], jnp.concatenate([at[i], c0[i]], axis=1)) for i in n]
        pmat = [_bdot(wu[i][:, :gl], bh[i], _TN) for i in n]
        qmat = [_bdot(jnp.concatenate([wu[i][:, gl:], vs[i]], axis=0),
                      jnp.concatenate([bh[i], kh[i]], axis=0), _TN) for i in n]
        ry = [_bdot(g_y[i], jnp.concatenate([wu[i], jnp.concatenate([jnp.zeros_like(vs[i]), vs[i]], axis=1)],
                                            axis=0)) for i in n]
        for i in n:
            cidx, g, g_chunk = items[i][:3]
            rw, y0 = rt[i] + ry[i][:, :gl], ry[i][:, gl:]
            for j in range(2):
                p = 2 * g + j
                lanes = slice(j * RWKV_PAIR, (j + 1) * RWKV_PAIR)
                head_rows = slice(2 * j * c, (2 * j + 2) * c)
                decay = g_chunk[:, p * RWKV_PAIR:(p + 1) * RWKV_PAIR]
                p_scr[cidx, p] = jnp.where(eye_pair, decay, 0.0) + pmat[i][lanes, lanes]
                q_scr[cidx, p] = qmat[i][lanes, lanes]
                rw_scr[cidx, p] = rw[head_rows, lanes]
                y0_scr[cidx, p] = y0[head_rows, lanes]
        return carry

    lax.fori_loop(0, nchunk // unroll, local, 0)
    s_ref[0] = s0_ref[0]

    def scan(ci, carry):
        cidx = (nchunk - 1 - ci) if reverse else ci
        rows = pl.ds(pl.multiple_of(cidx * c, c), c)
        s = [s_ref[0, p] for p in range(pairs)]
        ys = [_bdot(rw_scr[cidx, p], s[p], _NT) for p in range(pairs)]
        s_new = [_bdot(s[p], p_scr[cidx, p]) for p in range(pairs)]
        for p in range(pairs):
            y = ys[p] + y0_scr[cidx, p]
            y_ref[0, rows, p * RWKV_PAIR:(p + 1) * RWKV_PAIR] = y[:c] + y[c:]
            s_ref[0, p] = s_new[p] + q_scr[cidx, p]
        return carry

    lax.fori_loop(0, nchunk, scan, 0)


def _rwkv_scan(r, lw, k, v, a, b, s0, *, reverse, pairs, unroll):
    nb, seq, _ = r.shape
    assert 4 * RWKV_CHUNK == RWKV_PAIR and seq % (RWKV_CHUNK * unroll) == 0
    assert pairs % 2 == 0 and N_PAIRS % pairs == 0
    nchunk = seq // RWKV_CHUNK
    tok = pl.BlockSpec((1, seq, pairs * RWKV_PAIR), lambda bi, pi: (bi, 0, pi))
    st = pl.BlockSpec((1, pairs, RWKV_PAIR, RWKV_PAIR), lambda bi, pi: (bi, pi, 0, 0))
    pair_mat = pltpu.VMEM((nchunk, pairs, RWKV_PAIR, RWKV_PAIR), F32)
    pair_rows = pltpu.VMEM((nchunk, pairs, 2 * RWKV_CHUNK, RWKV_PAIR), F32)
    return pl.pallas_call(
        functools.partial(_rwkv_kernel, seq=seq, pairs=pairs, reverse=reverse, unroll=unroll),
        grid=(nb, N_PAIRS // pairs),
        in_specs=[tok] * 6 + [st],
        out_specs=[tok, st],
        out_shape=[jax.ShapeDtypeStruct(r.shape, F32), jax.ShapeDtypeStruct(s0.shape, F32)],
        scratch_shapes=[pair_mat, pair_mat, pair_rows, pair_rows],
        compiler_params=_cparams("parallel", "parallel"),
        name="rwkv_rev" if reverse else "rwkv_fwd",
    )(r, lw, k, v, a, b, s0)


def _moe_kernel(be_ref, nv_ref, cnt_ref, idx_ref, w_ref, h_hbm, wg_ref, wu_ref, wd_ref, y_hbm,
                xbuf, gacc, uacc, hbuf, acc, gsem, ssem):
    i = pl.program_id(0)
    f = pl.program_id(1)
    n_used = nv_ref[0]
    cur = i % 2

    def row_in(blk, r, dst):
        tok = idx_ref[blk * MOE_ROWS + r] & MOE_TOK_MASK
        return pltpu.make_async_copy(h_hbm.at[pl.ds(tok, 1)], xbuf.at[dst, pl.ds(r, 1)], gsem.at[dst])

    def row_out(r):
        slot = idx_ref[i * MOE_ROWS + r] >> MOE_TOK_BITS
        return pltpu.make_async_copy(acc.at[pl.ds(r, 1)], y_hbm.at[pl.ds(slot, 1)], ssem)

    def groups(blk):
        return (cnt_ref[blk] + MOE_DMA_GROUP - 1) // MOE_DMA_GROUP

    def gather(blk, dst):
        def body(g, carry):
            for r in range(MOE_DMA_GROUP):
                row_in(blk, g * MOE_DMA_GROUP + r, dst).start()
            return carry
        lax.fori_loop(0, groups(blk), body, 0)

    @pl.when((f == 0) & (i == 0))
    def _():
        xbuf[...] = jnp.zeros_like(xbuf)

    @pl.when((f == 0) & (i == 0) & (n_used > 0))
    def _():
        gather(0, 0)

    @pl.when((f == 0) & (i < n_used))
    def _():
        n = pl.multiple_of(groups(i) * MOE_DMA_GROUP, MOE_DMA_GROUP)
        pltpu.make_async_copy(h_hbm.at[pl.ds(0, n)], xbuf.at[cur, pl.ds(0, n)], gsem.at[cur]).wait()

        @pl.when(i + 1 < n_used)
        def _():
            gather(i + 1, 1 - cur)

    half_d = xbuf.shape[2] // 2
    half_ff = hbuf.shape[1] // 2

    def gate_up(lo):
        x = xbuf[cur, :, lo:lo + half_d].astype(BF16)
        return (jnp.dot(x, wg_ref[...].astype(BF16), preferred_element_type=F32),
                jnp.dot(x, wu_ref[...].astype(BF16), preferred_element_type=F32))

    def down(lo):
        return jnp.dot(hbuf[:, lo:lo + half_ff], wd_ref[...].astype(BF16), preferred_element_type=F32)

    @pl.when((i < n_used) & (f == 0))
    def _():
        gacc[...], uacc[...] = gate_up(0)

    @pl.when((i < n_used) & (f == 1))
    def _():
        g, u = gate_up(half_d)
        g, u = gacc[...] + g, uacc[...] + u
        hbuf[...] = ((g * jax.nn.sigmoid(g)) * u).astype(BF16)
        acc[...] = down(0)

    @pl.when((i < n_used) & (f == 2))
    def _():
        acc[...] = (acc[...] + down(half_ff)) * w_ref[...]

        n = cnt_ref[i]
        whole = n // MOE_DMA_GROUP

        def body(g, carry):
            for r in range(MOE_DMA_GROUP):
                row_out(g * MOE_DMA_GROUP + r).start()
            return carry
        lax.fori_loop(0, whole, body, 0)

        def rest(r, carry):
            row_out(r).start()
            return carry
        lax.fori_loop(whole * MOE_DMA_GROUP, n, rest, 0)

        @pl.when(whole > 0)
        def _():
            m = pl.multiple_of(whole * MOE_DMA_GROUP, MOE_DMA_GROUP)
            pltpu.make_async_copy(acc.at[pl.ds(0, m)], y_hbm.at[pl.ds(0, m)], ssem).wait()

        def rest_wait(r, carry):
            row_out(r).wait()
            return carry
        lax.fori_loop(whole * MOE_DMA_GROUP, n, rest_wait, 0)


def _moe_experts(layer, block_e, n_used, row_cnt, row_idx, row_w, h, w_gate, w_up, w_down):
    n_blocks = block_e.shape[0]
    n_tok, d = h.shape
    ff = w_gate.shape[-1]
    in_half = pl.BlockSpec((None, None, d // 2, ff),
                           lambda i, f, be, nv, ct, ix: (layer, be[i], jnp.minimum(f, 1), 0))
    return pl.pallas_call(
        _moe_kernel,
        grid_spec=pltpu.PrefetchScalarGridSpec(
            num_scalar_prefetch=4,
            grid=(n_blocks, 3),
            in_specs=[pl.BlockSpec((MOE_ROWS, 1), lambda i, f, be, nv, ct, ix: (i, 0)),
                      pl.BlockSpec(memory_space=pl.ANY),
                      in_half, in_half,
                      pl.BlockSpec((None, None, ff // 2, d),
                                   lambda i, f, be, nv, ct, ix: (layer, be[i], jnp.maximum(f - 1, 0), 0))],
            out_specs=pl.BlockSpec(memory_space=pl.ANY),
            scratch_shapes=[pltpu.VMEM((2, MOE_ROWS, d), F32), pltpu.VMEM((MOE_ROWS, ff), F32),
                            pltpu.VMEM((MOE_ROWS, ff), F32), pltpu.VMEM((MOE_ROWS, ff), BF16),
                            pltpu.VMEM((MOE_ROWS, d), F32),
                            pltpu.SemaphoreType.DMA((2,)), pltpu.SemaphoreType.DMA(())]),
        out_shape=jax.ShapeDtypeStruct((n_tok * TOP_K, d), F32),
        compiler_params=_cparams("arbitrary", "arbitrary"),
        name="moe_experts",
    )(block_e, n_used, row_cnt, row_idx, row_w, h, w_gate, w_up, w_down)


def _moe(layer, h_f32, w_router, b_router, w_gate, w_up, w_down):
    n_tok = h_f32.shape[0]
    logits = _mm(h_f32, w_router, precision=HIGHEST, tm=256, tn=LANES, name="router")
    logits = logits[:, :N_GROUPS + N_EXPERTS] + b_router
    gp = jax.nn.softmax(logits[:, :N_GROUPS], axis=-1)
    g_idx = jnp.argmax(gp, axis=-1)[:, None]
    p_g = jnp.max(gp, axis=-1, keepdims=True)
    el = logits[:, N_GROUPS:].reshape(-1, N_GROUPS, EXPERTS_PER_GROUP)
    el = jnp.take_along_axis(el, g_idx[:, :, None], axis=1)[:, 0]
    pe = jax.nn.softmax(el, axis=-1)
    e_first = jnp.argmax(pe, axis=-1)[:, None]
    pe_rest = jnp.where(jnp.arange(EXPERTS_PER_GROUP)[None, :] == e_first, -jnp.inf, pe)
    e_loc = jnp.concatenate([e_first, jnp.argmax(pe_rest, axis=-1)[:, None]], axis=1)
    p_e = jnp.concatenate([jnp.max(pe, axis=-1, keepdims=True), jnp.max(pe_rest, axis=-1, keepdims=True)], axis=1)
    wts = p_g * p_e / jnp.sum(p_e, axis=-1, keepdims=True)
    idx = g_idx * EXPERTS_PER_GROUP + e_loc

    n_slots = n_tok * TOP_K
    assert n_tok <= MOE_TOK_MASK and n_slots % MOE_ROWS == 0
    flat_e = idx.reshape(-1).astype(jnp.int32)
    order = jnp.argsort(flat_e).astype(jnp.int32)
    sorted_e = flat_e[order]
    counts = jnp.bincount(flat_e, length=N_EXPERTS)
    padded = (counts + MOE_ROWS - 1) // MOE_ROWS * MOE_ROWS
    pad_end = jnp.cumsum(padded)
    pad_start = pad_end - padded
    start = jnp.cumsum(counts) - counts
    dest_sorted = (pad_start[sorted_e] + jnp.arange(n_slots) - start[sorted_e]).astype(jnp.int32)
    n_blocks = n_slots // MOE_ROWS + N_EXPERTS
    n_rows = n_blocks * MOE_ROWS
    tok = order // TOP_K
    slot = (order % TOP_K) * n_tok + tok
    row_idx = jnp.zeros((n_rows,), jnp.int32).at[dest_sorted].set(tok | (slot << MOE_TOK_BITS))
    row_w = jnp.zeros((n_rows,), F32).at[dest_sorted].set(wts.reshape(-1)[order])
    blk_start = jnp.arange(n_blocks) * MOE_ROWS
    block_e = jnp.minimum(jnp.searchsorted(pad_end, blk_start, side='right'), N_EXPERTS - 1).astype(jnp.int32)
    row_cnt = jnp.clip(counts[block_e] - (blk_start - pad_start[block_e]), 0, MOE_ROWS).astype(jnp.int32)
    n_used = (pad_end[-1] // MOE_ROWS).astype(jnp.int32).reshape(1)
    return _moe_experts(layer, block_e, n_used, row_cnt, row_idx, row_w.reshape(n_rows, 1), h_f32,
                        w_gate, w_up, w_down)


def _axial_rope_tables(n_tok, rot_dim):
    t = jnp.arange(n_tok)
    rows = (t // GRID_W).astype(F32)
    cols = (t % GRID_W).astype(F32)
    axis_dim = rot_dim // 2
    inv = ROPE_BASE ** (-jnp.arange(0, axis_dim, 2, dtype=F32) / axis_dim)
    ang_r = rows[:, None] * inv[None, :]
    ang_c = cols[:, None] * inv[None, :]
    return jnp.cos(ang_r), jnp.sin(ang_r), jnp.cos(ang_c), jnp.sin(ang_c)


def _rope_rotate(x, cos, sin):
    half = x.shape[-1] // 2
    c, s = cos[None, :, None, :], sin[None, :, None, :]
    x1, x2 = x[..., :half], x[..., half:]
    return jnp.concatenate([x1 * c - x2 * s, x2 * c + x1 * s], axis=-1)


def _apply_axial_rope(x, tables):
    cos_r, sin_r, cos_c, sin_c = tables
    h = x.shape[-1] // 2
    return jnp.concatenate([_rope_rotate(x[..., :h], cos_r, sin_r),
                            _rope_rotate(x[..., h:], cos_c, sin_c)], axis=-1)


def _conv_kernel(b_ref, c_ref, u_ref, cp_ref, up_ref, cn_ref, un_ref, w_ref, o_ref, *, tm):
    i = pl.program_id(0)
    nc = N_CTX_TOK // tm
    in_seq = jnp.where(i < nc, i % (SEQ // tm), (i - nc) % (DEC_SEQ // tm))
    last_in_seq = jnp.where(i < nc, SEQ // tm - 1, DEC_SEQ // tm - 1)
    z = c_ref[...] * u_ref[...]
    z_before = jnp.where(in_seq == 0, 0.0, cp_ref[7:8, :] * up_ref[7:8, :])
    z_after = jnp.where(in_seq == last_in_seq, 0.0, cn_ref[0:1, :] * un_ref[0:1, :])
    row = lax.broadcasted_iota(jnp.int32, (tm, 1), 0)
    z_prev = jnp.where(row == 0, z_before, pltpu.roll(z, 1, axis=0))
    z_next = jnp.where(row == tm - 1, z_after, pltpu.roll(z, tm - 1, axis=0))
    conv = w_ref[0:1, :] * z_prev + w_ref[1:2, :] * z + w_ref[2:3, :] * z_next
    o_ref[...] = (b_ref[...] * conv).astype(o_ref.dtype)


def _short_conv(p, col0, conv_w, tm=128, tw=1024):
    assert col0 % tw == 0 and B_WIDTH % tw == 0 and SEQ % tm == 0 and tm % 8 == 0
    c0, nw, g8, last8 = col0 // tw, B_WIDTH // tw, tm // 8, p.shape[0] // 8 - 1
    tile = lambda n: pl.BlockSpec((tm, tw), lambda i, c: (i, c0 + n * nw + c))
    before = lambda n: pl.BlockSpec((8, tw), lambda i, c: (jnp.maximum(i * g8 - 1, 0), c0 + n * nw + c))
    after = lambda n: pl.BlockSpec((8, tw), lambda i, c: (jnp.minimum((i + 1) * g8, last8), c0 + n * nw + c))
    return pl.pallas_call(
        functools.partial(_conv_kernel, tm=tm),
        grid=(p.shape[0] // tm, nw),
        in_specs=[tile(0), tile(1), tile(2), before(1), before(2), after(1), after(2),
                  pl.BlockSpec((3, tw), lambda i, c: (0, c))],
        out_specs=pl.BlockSpec((tm, tw), lambda i, c: (i, c)),
        out_shape=jax.ShapeDtypeStruct((p.shape[0], B_WIDTH), BF16),
        compiler_params=_cparams("parallel", "parallel"),
        name="short_conv",
    )(p, p, p, p, p, p, p, conv_w)


def _out_proj_kernel(a1_ref, a2_ref, w_ref, res_ref, gate_ref, o_ref):
    k1 = a1_ref.shape[1]
    w = w_ref[...].astype(BF16)
    acc = (jnp.dot(a1_ref[...], w[:k1], preferred_element_type=F32) +
           jnp.dot(a2_ref[...], w[k1:], preferred_element_type=F32))
    o_ref[...] = res_ref[...] + gate_ref[0] * acc


def _out_proj(a1, a2, w, w_idx, res, gate, tm=1024, tn=512, name="out_proj"):
    m, k1 = a1.shape
    k2 = a2.shape[1]
    n = w.shape[-1]
    assert w.shape[-2] == k1 + k2 and m % tm == 0 and n % tn == 0
    lead = tuple(w_idx)
    return pl.pallas_call(
        _out_proj_kernel,
        grid=(m // tm, n // tn),
        in_specs=[pl.BlockSpec((tm, k1), lambda i, j: (i, 0)), pl.BlockSpec((tm, k2), lambda i, j: (i, 0)),
                  pl.BlockSpec((None,) * len(lead) + (k1 + k2, tn), lambda i, j: lead + (0, j)),
                  pl.BlockSpec((tm, tn), lambda i, j: (i, j)),
                  pl.BlockSpec((1, 1, tn), lambda i, j: (_row_group(i, tm), 0, j))],
        out_specs=pl.BlockSpec((tm, tn), lambda i, j: (i, j)),
        out_shape=jax.ShapeDtypeStruct((m, n), F32),
        compiler_params=_cparams("parallel", "parallel"),
        name=name,
    )(a1, a2, w, res, gate.reshape(N_MOD_GROUPS, 1, n))


def _even_mixer(h, i, cache_k, cache_v, w_in, w_out, sink, conv_w, x, gate):
    p = _mm(h, w_in, (i,), name="even_in")
    o_k, o_v, o_b = A_Q_W, A_Q_W + A_KV_W, A_Q_W + 2 * A_KV_W
    att_c = _gqa_attn(sink[i], (p, 0, 0), (p, 0, o_k), (p, 0, o_v),
                      nb=BATCH, tq_len=SEQ, band=False, out_rows=N_CTX_TOK)
    pl_ = p[N_CTX_TOK:]
    tabs = _axial_rope_tables(DEC_SEQ, HEAD_DIM)
    q_rot = _apply_axial_rope(pl_[:, :A_Q_W].reshape(DEC_BATCH, DEC_SEQ, A_HEADS, HEAD_DIM), tabs)
    k_rot = _apply_axial_rope(pl_[:, o_k:o_v].reshape(DEC_BATCH, DEC_SEQ, A_KV_HEADS, HEAD_DIM), tabs)
    ck = cache_k[:, i].reshape(DEC_BATCH * PAST_LEN, A_KV_W)
    cv = cache_v[:, i].reshape(DEC_BATCH * PAST_LEN, A_KV_W)
    att_l = _gqa_attn(sink[i], (q_rot.reshape(N_LAT_TOK, A_Q_W), 0, 0),
                      (k_rot.reshape(N_LAT_TOK, A_KV_W), 0, 0), (p, N_CTX_TOK, o_v),
                      ((p, N_CTX_TOK, 0), (ck, 0, 0), (cv, 0, 0), PAST_LEN),
                      nb=DEC_BATCH, tq_len=DEC_SEQ, band=True, out_rows=N_LAT_TOK)
    conv = _short_conv(p, o_b, conv_w[i])
    x = _out_proj(jnp.concatenate([att_c, att_l], axis=0), conv, w_out, (i,), x, gate, name="even_out")
    new_k = p[:N_CTX_TOK, o_k:o_v].reshape(BATCH, SEQ, A_KV_HEADS, HEAD_DIM)
    new_v = p[:N_CTX_TOK, o_v:o_b].reshape(BATCH, SEQ, A_KV_HEADS, HEAD_DIM)
    return x, new_k, new_v


def _rms(x, g):
    return x * lax.rsqrt(jnp.mean(x * x, axis=-1, keepdims=True) + NORM_EPS) * g


def _head_sums(x):
    rows, width = x.shape
    nt = width // LANES
    ri = lax.broadcasted_iota(jnp.int32, (LANES, LANES), 0) // D_HEAD_SIZE
    ci = lax.broadcasted_iota(jnp.int32, (LANES, LANES), 1) // D_HEAD_SIZE
    same_head = (ri == ci).astype(BF16)
    stacked = jnp.concatenate([part[:, j * LANES:(j + 1) * LANES] for part in _bf16_parts(x, 3) for j in range(nt)],
                              axis=0)
    s = jnp.dot(stacked, same_head, preferred_element_type=F32)
    n = nt * rows
    s = s[:n] + (s[n:2 * n] + s[2 * n:])
    return jnp.concatenate([s[j * rows:(j + 1) * rows] for j in range(nt)], axis=1)


def _rwkv_prep_kernel(x_ref, edge_ref, mu_ref, w0_ref, a0_ref, kk_ref, ka_ref, wup_ref, aup_ref,
                      r_ref, lw_ref, k_ref, v_ref, a_ref, b_ref, *, tm, seq, reverse):
    j = pl.program_id(0)
    blocks_per_seq = seq // tm
    x = x_ref[...]
    row = lax.broadcasted_iota(jnp.int32, (tm, 1), 0)
    if reverse:
        edge = jnp.where(j % blocks_per_seq == blocks_per_seq - 1, 0.0, edge_ref[0:1, :])
        nbr = jnp.where(row == tm - 1, edge, pltpu.roll(x, tm - 1, axis=0))
    else:
        edge = jnp.where(j % blocks_per_seq == 0, 0.0, edge_ref[7:8, :])
        nbr = jnp.where(row == 0, edge, pltpu.roll(x, 1, axis=0))
    z = x + mu_ref[...] * (nbr - x)
    r, k, v = (z[:, n * D_WIDTH:(n + 1) * D_WIDTH] for n in range(3))
    tail = z[:, 3 * D_WIDTH:]
    w_raw = w0_ref[...] + _bdot(jnp.tanh(tail), wup_ref[...])
    sp = jnp.maximum(-w_raw, 0.0) + jnp.log(1.0 + jnp.exp(-jnp.abs(w_raw)))
    lw_ref[...] = -jnp.exp(-sp - 0.5)
    a = jax.nn.sigmoid(a0_ref[...] + _bdot(tail, aup_ref[...]))
    kk = k * kk_ref[...]
    kk = kk / jnp.maximum(jnp.sqrt(_head_sums(kk * kk)), 1e-12)
    r_ref[...] = r
    k_ref[...] = k * (1 + (a - 1) * ka_ref[...])
    v_ref[...] = v
    a_ref[...] = -kk
    b_ref[...] = kk * a


def _rwkv_prep(rw, row0, nb, seq, mu, w0, a0, k_k, k_a, wup_pad, aup_pad, *, reverse, tm=128):
    rows = nb * seq
    assert row0 % tm == 0 and seq % tm == 0 and tm % 8 == 0
    b0, g8 = row0 // tm, tm // 8
    if reverse:
        edge = pl.BlockSpec((8, SHIFT_W), lambda j: (jnp.minimum((b0 + j + 1) * g8, rw.shape[0] // 8 - 1), 0))
    else:
        edge = pl.BlockSpec((8, SHIFT_W), lambda j: (jnp.maximum((b0 + j) * g8 - 1, 0), 0))
    vec = lambda n: pl.BlockSpec((1, n), lambda j: (0, 0))
    lora = pl.BlockSpec((LANES, D_WIDTH), lambda j: (0, 0))
    out = pl.BlockSpec((tm, D_WIDTH), lambda j: (j, 0))
    return pl.pallas_call(
        functools.partial(_rwkv_prep_kernel, tm=tm, seq=seq, reverse=reverse),
        grid=(rows // tm,),
        in_specs=[pl.BlockSpec((tm, SHIFT_W), lambda j: (b0 + j, 0)), edge, vec(SHIFT_W)] + [vec(D_WIDTH)] * 4 +
                 [lora, lora],
        out_specs=[out] * 6,
        out_shape=[jax.ShapeDtypeStruct((rows, D_WIDTH), F32)] * 6,
        compiler_params=_cparams("parallel"),
        name="rwkv_prep",
    )(rw, rw, mu.reshape(1, -1), w0.reshape(1, -1), a0.reshape(1, -1), k_k.reshape(1, -1), k_a.reshape(1, -1),
      wup_pad, aup_pad)


def _rwkv_readout_kernel(*refs):
    dirs = (refs[0:4], refs[4:8])
    glo_ref, gup_ref, rk_ref, lnw_ref, lnb_ref, o_ref = refs[8:]
    total = None
    for d, (y_ref, r_ref, k_ref, v_ref) in enumerate(dirs):
        y = y_ref[...]
        dev = y - _head_sums(y) * (1.0 / D_HEAD_SIZE)
        var = _head_sums(dev * dev) * (1.0 / D_HEAD_SIZE)
        yn = dev * lax.rsqrt(var + GN_EPS) * lnw_ref[...] + lnb_ref[...]
        out = yn + _head_sums(r_ref[...] * k_ref[...] * rk_ref[d:d + 1, :]) * v_ref[...]
        total = out if total is None else total + out
    gate = _bdot(jax.nn.sigmoid(glo_ref[...]), gup_ref[...])
    o_ref[...] = (total * gate).astype(o_ref.dtype)


def _rwkv_readout(per_dir, g_lo, row0, g_up, r_k, ln_w, ln_b, tm=128):
    rows = per_dir[0][0].shape[0]
    tok = pl.BlockSpec((tm, D_WIDTH), lambda j: (j, 0))
    vec = pl.BlockSpec((1, D_WIDTH), lambda j: (0, 0))
    return pl.pallas_call(
        _rwkv_readout_kernel,
        grid=(rows // tm,),
        in_specs=[tok] * 8 + [pl.BlockSpec((tm, GATE_LORA), lambda j: (row0 // tm + j, 0)),
                              pl.BlockSpec((GATE_LORA, D_WIDTH), lambda j: (0, 0)),
                              pl.BlockSpec((2, D_WIDTH), lambda j: (0, 0)), vec, vec],
        out_specs=tok,
        out_shape=jax.ShapeDtypeStruct((rows, D_WIDTH), BF16),
        compiler_params=_cparams("parallel"),
        name="rwkv_readout",
    )(*per_dir[0], *per_dir[1], g_lo, g_up, r_k.reshape(2, D_WIDTH), ln_w.reshape(1, -1), ln_b.reshape(1, -1))


def _rwkv_group(rw, g_lo, row0, s0, i, rp, nb, seq, pairs):
    per_dir, finals = [], []
    zeros = jnp.zeros((LANES - DECAY_LORA, D_WIDTH), F32)
    for d in range(2):
        wup_pad = jnp.concatenate([rp['w_up'][i, d], zeros], axis=0)
        aup_pad = jnp.concatenate([zeros, rp['a_up'][i, d]], axis=0)
        r, lw, k, v, a, b = _rwkv_prep(rw, row0, nb, seq, rp['mu'][i, d], rp['w0'][i, d], rp['a0'][i, d],
                                       rp['k_k'][i, d], rp['k_a'][i, d], wup_pad, aup_pad, reverse=(d == 1))
        to3 = lambda t: t.reshape(nb, seq, D_WIDTH)
        sd = s0[:, d].reshape(nb, N_PAIRS, 2, D_HEAD_SIZE, D_HEAD_SIZE)
        eye2 = jnp.eye(2, dtype=F32)
        s_bd = jnp.einsum('bphij,hg->bphigj', sd, eye2).reshape(nb, N_PAIRS, RWKV_PAIR, RWKV_PAIR)
        y, s_fin = _rwkv_scan(to3(r), to3(lw), to3(k), to3(v), to3(a), to3(b), s_bd,
                              reverse=(d == 1), pairs=pairs, unroll=16 // pairs)
        s_fin = s_fin.reshape(nb, N_PAIRS, 2, D_HEAD_SIZE, 2, D_HEAD_SIZE)
        s_fin = jnp.stack([s_fin[:, :, 0, :, 0], s_fin[:, :, 1, :, 1]], axis=2)
        finals.append(s_fin.reshape(nb, D_HEADS, D_HEAD_SIZE, D_HEAD_SIZE))
        per_dir.append((y.reshape(nb * seq, D_WIDTH), r, k, v))
    out = _rwkv_readout(per_dir, g_lo, row0, rp['g_up'][i], rp['r_k'][i], rp['ln_w'][i], rp['ln_b'][i])
    return out, jnp.stack(finals, axis=1)


def _odd_mixer(h, i, cache_ckv, cache_kpe, state, w_in_t, w_out, mp, rp, x, gate):
    p = _mm(h, w_in_t, (i,), w_t=True, name="odd_in")
    o1 = Q_LORA
    o2 = o1 + KV_LORA
    o3 = o2 + QK_ROPE
    o4 = o3 + SHIFT_W
    q_down, kv_down, k_pe, rw, g_lo = p[:, :o1], p[:, o1:o2], p[:, o2:o3], p[:, o3:o4], p[:, o4:]
    w_uq = mp['w_uq'][i].reshape(Q_LORA, C_HEADS, QK_NOPE + QK_ROPE)
    w_uq = jnp.concatenate([w_uq[:, :, :QK_NOPE].reshape(Q_LORA, -1), w_uq[:, :, QK_NOPE:].reshape(Q_LORA, -1)],
                           axis=1)
    w_ukv = mp['w_ukv'][i].reshape(KV_LORA, C_HEADS, QK_NOPE + V_DIM)
    w_ukv = jnp.concatenate([w_ukv[:, :, :QK_NOPE].reshape(KV_LORA, -1), w_ukv[:, :, QK_NOPE:].reshape(KV_LORA, -1)],
                            axis=1)
    q = _mm(_rms(q_down, mp['q_norm'][i]), w_uq, name="mla_uq")
    c_kv = _rms(kv_down, mp['kv_norm'][i])
    kv = _mm(c_kv, w_ukv, name="mla_ukv")
    kv_ctx = _mm(cache_ckv[:, i].reshape(DEC_BATCH * PAST_LEN, KV_LORA), w_ukv, name="mla_ukv_cache")
    n_nope = C_HEADS * QK_NOPE
    dup = lambda t: jnp.concatenate([t, t], axis=-1)
    kp = dup(k_pe)
    att_c = _mla_attn((q, 0, 0), (q, 0, n_nope), (kv, 0, 0), (kp, 0, 0), (kv, 0, n_nope),
                      nb=BATCH, tq_len=SEQ, out_rows=N_CTX_TOK)
    tabs = _axial_rope_tables(DEC_SEQ, QK_ROPE)
    qpe_rot = _apply_axial_rope(q[N_CTX_TOK:, n_nope:].reshape(DEC_BATCH, DEC_SEQ, C_HEADS, QK_ROPE), tabs)
    kpe_rot = _apply_axial_rope(k_pe[N_CTX_TOK:].reshape(DEC_BATCH, DEC_SEQ, 1, QK_ROPE), tabs)
    kp_ctx = dup(cache_kpe[:, i].reshape(DEC_BATCH * PAST_LEN, QK_ROPE))
    att_l = _mla_attn((q, N_CTX_TOK, 0), (qpe_rot.reshape(N_LAT_TOK, -1), 0, 0), (kv, N_CTX_TOK, 0),
                      (dup(kpe_rot.reshape(N_LAT_TOK, QK_ROPE)), 0, 0), (kv, N_CTX_TOK, n_nope),
                      ((q, N_CTX_TOK, n_nope), (kv_ctx, 0, 0), (kp_ctx, 0, 0), (kv_ctx, 0, n_nope), PAST_LEN),
                      nb=DEC_BATCH, tq_len=DEC_SEQ, out_rows=N_LAT_TOK)
    s_zero = jnp.zeros((BATCH, 2, D_HEADS, D_HEAD_SIZE, D_HEAD_SIZE), F32)
    rw_c, s_c = _rwkv_group(rw, g_lo, 0, s_zero, i, rp, BATCH, SEQ, 4)
    rw_l, _ = _rwkv_group(rw, g_lo, N_CTX_TOK, state[:, i].astype(F32), i, rp, DEC_BATCH, DEC_SEQ, 2)
    x = _out_proj(jnp.concatenate([att_c, att_l], axis=0), jnp.concatenate([rw_c, rw_l], axis=0), w_out, (i,),
                  x, gate, name="odd_out")
    new_ckv = c_kv[:N_CTX_TOK].reshape(BATCH, SEQ, KV_LORA)
    new_kpe = k_pe[:N_CTX_TOK].reshape(BATCH, SEQ, QK_ROPE)
    return x, new_ckv, new_kpe, s_c


def kernel(x_prompt, x_sample, c, c_ctx, cache_attn_k, cache_attn_v, cache_mla_ckv, cache_mla_kpe,
           state_rwkv, w_mod, b_mod, norm_mix, norm_ffn, norm_final, even_w_in, even_w_out, attn_sink,
           conv_w, odd_w_in, odd_w_out, mla_q_norm, mla_kv_norm, mla_w_uq, mla_w_ukv, rwkv_mu, rwkv_w0,
           rwkv_w_up, rwkv_a0, rwkv_a_up, rwkv_k_k, rwkv_k_a, rwkv_r_k, rwkv_g_up, rwkv_ln_w, rwkv_ln_b,
           router_group_w, router_group_b, router_expert_w, router_expert_b, expert_w_gate, expert_w_up,
           expert_w_down):
    d = D_MODEL
    x = jnp.concatenate([x_prompt.reshape(N_CTX_TOK, d), x_sample.reshape(N_LAT_TOK, d)], axis=0)
    cvec = jax.nn.silu(jnp.concatenate([c_ctx[None], c], axis=0))
    cvec = jnp.pad(cvec, ((0, 8 - N_MOD_GROUPS), (0, 0)))
    mp = dict(q_norm=mla_q_norm, kv_norm=mla_kv_norm, w_uq=mla_w_uq, w_ukv=mla_w_ukv)
    rp = dict(mu=rwkv_mu, w0=rwkv_w0, w_up=rwkv_w_up, a0=rwkv_a0, a_up=rwkv_a_up, k_k=rwkv_k_k,
              k_a=rwkv_k_a, r_k=rwkv_r_k, g_up=rwkv_g_up, ln_w=rwkv_ln_w, ln_b=rwkv_ln_b)
    odd_w_in_t = jnp.swapaxes(odd_w_in, 1, 2)
    new_k, new_v, new_ckv, new_kpe, new_s = [], [], [], [], []
    ffn = None
    for l in range(DEPTH):
        mod = _mm(cvec, w_mod, (l,), tn=1024, tk=2048, name="modulation")[:N_MOD_GROUPS] + b_mod[l]
        shift_mix, scale_mix, gate_mix, shift_ffn, scale_ffn, gate_ffn = jnp.split(mod, 6, axis=-1)
        if ffn is None:
            (h,) = _normmod(x, norm_mix[l], shift_mix, scale_mix, (BF16,))
        else:
            x, h = _normmod(x, norm_mix[l], shift_mix, scale_mix, (BF16,), res=ffn, emit_x=True)
        i = l // 2
        if l % 2 == 0:
            x, k_c, v_c = _even_mixer(h, i, cache_attn_k, cache_attn_v, even_w_in, even_w_out, attn_sink,
                                      conv_w, x, gate_mix)
            new_k.append(k_c)
            new_v.append(v_c)
        else:
            x, ckv_c, kpe_c, s_c = _odd_mixer(h, i, cache_mla_ckv, cache_mla_kpe, state_rwkv, odd_w_in_t,
                                              odd_w_out, mp, rp, x, gate_mix)
            new_ckv.append(ckv_c)
            new_kpe.append(kpe_c)
            new_s.append(s_c)
        (h_ffn,) = _normmod(x, norm_ffn[l], shift_ffn, scale_ffn, (F32,))
        w_router = jnp.pad(jnp.concatenate([router_group_w[l], router_expert_w[l]], axis=1),
                           ((0, 0), (0, LANES - N_GROUPS - N_EXPERTS)))
        b_router = jnp.concatenate([router_group_b[l], router_expert_b[l]])
        ffn = (_moe(l, h_ffn, w_router, b_router, expert_w_gate, expert_w_up, expert_w_down), gate_ffn)
    zeros = jnp.zeros((N_MOD_GROUPS, d), F32)
    (y_all,) = _normmod(x, norm_final, zeros, zeros, (F32,), res=ffn)
    return (y_all[:N_CTX_TOK].reshape(BATCH, SEQ, d), y_all[N_CTX_TOK:].reshape(DEC_BATCH, DEC_SEQ, d),
            jnp.stack(new_k, axis=1), jnp.stack(new_v, axis=1), jnp.stack(new_ckv, axis=1),
            jnp.stack(new_kpe, axis=1), jnp.stack(new_s, axis=1))
```

```python
import functools

import jax
import jax.numpy as jnp
from jax import lax
from jax.experimental import pallas as pl
from jax.experimental.pallas import tpu as pltpu

F32 = jnp.float32
BF16 = jnp.bfloat16

D_MODEL = 4096
BATCH = 16
SEQ = 256
DEPTH = 4
DEC_BATCH = 2
DEC_SEQ = 1024
PAST_LEN = 512
GRID_W = 64
WINDOW = 128
ROPE_BASE = 10000.0
NORM_EPS = 1e-6
NEG_INF = -1e30

HEAD_DIM = 128
A_HEADS = 16
A_KV_HEADS = 4
A_GROUP = A_HEADS // A_KV_HEADS
A_Q_W = A_HEADS * HEAD_DIM
A_KV_W = A_KV_HEADS * HEAD_DIM
B_WIDTH = D_MODEL // 2
C_HEADS = 16
Q_LORA = D_MODEL // 4
KV_LORA = D_MODEL // 8
QK_NOPE = 128
QK_ROPE = 64
V_DIM = 128
C_OUT_W = C_HEADS * V_DIM
D_WIDTH = D_MODEL // 2
D_HEAD_SIZE = 64
D_HEADS = D_WIDTH // D_HEAD_SIZE
DECAY_LORA = 64
AAA_LORA = 64
GATE_LORA = 256
GN_EPS = 64e-5
SHIFT_W = 3 * D_WIDTH + DECAY_LORA + AAA_LORA
N_GROUPS = 8
EXPERTS_PER_GROUP = 8
N_EXPERTS = N_GROUPS * EXPERTS_PER_GROUP
TOP_K = 2
EXPERT_FF = 512

N_CTX_TOK = BATCH * SEQ
N_LAT_TOK = DEC_BATCH * DEC_SEQ
N_TOK = N_CTX_TOK + N_LAT_TOK
N_MOD_GROUPS = 1 + DEC_BATCH

LANES = 128
VMEM_LIMIT_BYTES = 56 * 1024 * 1024
MOE_ROWS = 256
MOE_DMA_GROUP = 8
MOE_TOK_BITS = 13
MOE_TOK_MASK = (1 << MOE_TOK_BITS) - 1
RWKV_CHUNK = 32
RWKV_PAIR = 2 * D_HEAD_SIZE
N_PAIRS = D_WIDTH // RWKV_PAIR


_NN = (((1,), (0,)), ((), ()))
_NT = (((1,), (1,)), ((), ()))
_TN = (((0,), (0,)), ((), ()))


def _cparams(*sem):
    return pltpu.CompilerParams(dimension_semantics=sem, vmem_limit_bytes=VMEM_LIMIT_BYTES)


def _row_group(i, tm):
    nc = N_CTX_TOK // tm
    nl = DEC_SEQ // tm
    return jnp.where(i < nc, 0, 1 + (i - nc) // nl)


def _normmod_kernel(*refs, has_res, emit_x, has_router):
    if has_res:
        x_ref, ya_ref, yb_ref, gate_ref, g_ref, shift_ref, scale_ref, *out_refs = refs
        x = x_ref[...] + gate_ref[0] * (ya_ref[...] + yb_ref[...])
    else:
        x_ref, g_ref, shift_ref, scale_ref, *out_refs = refs
        x = x_ref[...]
    if has_router:
        wr_ref, *out_refs = out_refs
    if emit_x:
        out_refs[0][...] = x
        out_refs = out_refs[1:]
    y = x * lax.rsqrt(jnp.mean(x * x, axis=-1, keepdims=True) + NORM_EPS)
    y = y * g_ref[...]
    y = y * (1 + scale_ref[0]) + shift_ref[0]
    if has_router:
        out_refs[-1][...] = jnp.dot(y, wr_ref[...], preferred_element_type=F32, precision=lax.Precision.HIGHEST)
        out_refs = out_refs[:-1]
    for o in out_refs:
        o[...] = y.astype(o.dtype)


def _normmod(x, g, shift, scale, out_dtypes, res=None, emit_x=False, router=None, tm=128):
    n, d = x.shape
    grp = lambda i: (_row_group(i, tm), 0, 0)
    row = pl.BlockSpec((tm, d), lambda i: (i, 0))
    vec = pl.BlockSpec((1, 1, d), grp)
    in_specs, args = [row], [x]
    if res is not None:
        y2, gate = res
        in_specs += [row, pl.BlockSpec((tm, d), lambda i: (n // tm + i, 0)), vec]
        args += [y2, y2, gate.reshape(-1, 1, d)]
    in_specs += [pl.BlockSpec((1, d), lambda i: (0, 0)), vec, vec]
    args += [g.reshape(1, d), shift.reshape(-1, 1, d), scale.reshape(-1, 1, d)]
    out_dtypes = ((F32,) if emit_x else ()) + tuple(out_dtypes)
    out_specs = [row for _ in out_dtypes]
    out_shape = [jax.ShapeDtypeStruct((n, d), dt) for dt in out_dtypes]
    if router is not None:
        in_specs.append(pl.BlockSpec(router.shape, lambda i: (0, 0)))
        args.append(router)
        out_specs.append(pl.BlockSpec((tm, router.shape[1]), lambda i: (i, 0)))
        out_shape.append(jax.ShapeDtypeStruct((n, router.shape[1]), F32))
    return pl.pallas_call(
        functools.partial(_normmod_kernel, has_res=res is not None, emit_x=emit_x, has_router=router is not None),
        grid=(n // tm,),
        in_specs=in_specs,
        out_specs=out_specs,
        out_shape=out_shape,
        compiler_params=_cparams("parallel"),
        name="normmod",
    )(*args)


def _mm_kernel(*refs, nk, has_res, precision, w_t):
    if has_res:
        a_ref, w_ref, res_ref, gate_ref, o_ref, acc_ref = refs
    else:
        a_ref, w_ref, o_ref, acc_ref = refs
    k = pl.program_id(2)
    a = a_ref[...]
    w = w_ref[...]
    if precision is None:
        a = a.astype(BF16)
        w = w.astype(BF16)
    part = lax.dot_general(a, w, _NT if w_t else _NN, preferred_element_type=F32, precision=precision)

    def finish(acc):
        if has_res:
            acc = res_ref[...] + gate_ref[0] * acc
        o_ref[...] = acc.astype(o_ref.dtype)

    if nk == 1:
        finish(part)
    else:
        @pl.when(k == 0)
        def _():
            acc_ref[...] = part

        @pl.when(k > 0)
        def _():
            acc_ref[...] += part

        @pl.when(k == nk - 1)
        def _():
            finish(acc_ref[...])


def _mm(a, w, w_idx=(), *, out_dtype=F32, res=None, gate=None, precision=None, w_t=False,
        tm=1024, tn=512, tk=4096, name="mm"):
    m, kdim = a.shape
    n = w.shape[-2] if w_t else w.shape[-1]
    assert (w.shape[-1] if w_t else w.shape[-2]) == kdim
    tm, tn, tk = min(tm, m), min(tn, n), min(tk, kdim)
    assert m % tm == 0 and kdim % tk == 0
    nk = kdim // tk
    lead = tuple(w_idx)
    if w_t:
        w_spec = pl.BlockSpec((None,) * len(lead) + (tn, tk), lambda i, j, k: lead + (j, k))
    else:
        w_spec = pl.BlockSpec((None,) * len(lead) + (tk, tn), lambda i, j, k: lead + (k, j))
    in_specs = [pl.BlockSpec((tm, tk), lambda i, j, k: (i, k)), w_spec]
    args = [a, w]
    has_res = res is not None
    if has_res:
        in_specs += [pl.BlockSpec((tm, tn), lambda i, j, k: (i, j)),
                     pl.BlockSpec((1, 1, tn), lambda i, j, k: (_row_group(i, tm), 0, j))]
        args += [res, gate.reshape(N_MOD_GROUPS, 1, n)]
    return pl.pallas_call(
        functools.partial(_mm_kernel, nk=nk, has_res=has_res, precision=precision, w_t=w_t),
        grid=(m // tm, pl.cdiv(n, tn), nk),
        in_specs=in_specs,
        out_specs=pl.BlockSpec((tm, tn), lambda i, j, k: (i, j)),
        out_shape=jax.ShapeDtypeStruct((m, n), out_dtype),
        scratch_shapes=[pltpu.VMEM((tm, tn), F32)],
        compiler_params=_cparams("parallel", "parallel", "arbitrary"),
        name=name,
    )(*args)


def _dot_nt(a, b, precision=None):
    return lax.dot_general(a, b, (((1,), (1,)), ((), ())), preferred_element_type=F32,
                           precision=precision)


def _softmax_read(scores, values, sink_col):
    m = scores[0].max(axis=-1, keepdims=True)
    for s in scores[1:]:
        m = jnp.maximum(m, s.max(axis=-1, keepdims=True))
    if sink_col is not None:
        m = jnp.maximum(m, sink_col)
    den = jnp.exp(sink_col - m) if sink_col is not None else 0.0
    acc = None
    for s, v in zip(scores, values):
        p = jnp.exp(s - m)
        den = den + p.sum(axis=-1, keepdims=True)
        pv = jnp.dot(p.astype(BF16), v, preferred_element_type=F32)
        acc = pv if acc is None else acc + pv
    return acc / den


def _gqa_kernel(*refs, tq, has2, band):
    if has2:
        sink_ref, q1_ref, k1_ref, v1_ref, q2_ref, k2_ref, v2_ref, o_ref = refs
    else:
        sink_ref, q1_ref, k1_ref, v1_ref, o_ref = refs
    h = pl.program_id(1)
    qi = pl.program_id(2)
    scale = HEAD_DIM ** -0.5

    def stack(q_ref):
        q = q_ref[...]
        return jnp.concatenate([q[:, g * HEAD_DIM:(g + 1) * HEAD_DIM] for g in range(A_GROUP)],
                               axis=0).astype(BF16)

    s1 = _dot_nt(stack(q1_ref), k1_ref[...].astype(BF16)) * scale
    if band:
        qpos = qi * tq + lax.broadcasted_iota(jnp.int32, s1.shape, 0) % tq
        kpos = lax.broadcasted_iota(jnp.int32, s1.shape, 1)
        s1 = jnp.where(jnp.abs(kpos - qpos) <= WINDOW, s1, NEG_INF)
    scores, values = [s1], [v1_ref[...].astype(BF16)]
    if has2:
        scores.append(_dot_nt(stack(q2_ref), k2_ref[...].astype(BF16)) * scale)
        values.append(v2_ref[...].astype(BF16))
    sink_col = jnp.concatenate([jnp.full((tq, 1), sink_ref[h * A_GROUP + g], F32)
                                for g in range(A_GROUP)], axis=0)
    out = _softmax_read(scores, values, sink_col)
    for g in range(A_GROUP):
        o_ref[:, g * HEAD_DIM:(g + 1) * HEAD_DIM] = out[g * tq:(g + 1) * tq].astype(o_ref.dtype)


def _tok_spec(rows, width, row0, rows_per_batch, col0, per_q):
    assert row0 % rows == 0 and rows_per_batch % rows == 0 and col0 % width == 0
    r0, rb, c0 = row0 // rows, rows_per_batch // rows, col0 // width
    if per_q:
        return pl.BlockSpec((rows, width), lambda b, h, qi: (r0 + b * rb + qi, c0 + h))
    return pl.BlockSpec((rows, width), lambda b, h, qi: (r0 + b * rb, c0 + h))


def _gqa_attn(sink, q1, k1, v1, seg2=None, *, nb, tq_len, band, out_rows, tq=128):
    gw = A_GROUP * HEAD_DIM
    qspec = lambda t: _tok_spec(tq, gw, t[1], tq_len, t[2], True)
    kspec = lambda t, tk: _tok_spec(tk, HEAD_DIM, t[1], tk, t[2], False)
    in_specs = [pl.BlockSpec(memory_space=pltpu.SMEM), qspec(q1), kspec(k1, tq_len), kspec(v1, tq_len)]
    args = [sink, q1[0], k1[0], v1[0]]
    if seg2 is not None:
        q2, k2, v2, tk2 = seg2
        in_specs += [qspec(q2), kspec(k2, tk2), kspec(v2, tk2)]
        args += [q2[0], k2[0], v2[0]]
    return pl.pallas_call(
        functools.partial(_gqa_kernel, tq=tq, has2=seg2 is not None, band=band),
        grid=(nb, A_KV_HEADS, tq_len // tq),
        in_specs=in_specs,
        out_specs=_tok_spec(tq, gw, 0, tq_len, 0, True),
        out_shape=jax.ShapeDtypeStruct((out_rows, A_Q_W), BF16),
        compiler_params=_cparams("parallel", "parallel", "arbitrary"),
        name="gqa_attn",
    )(*args)


def _mla_kernel(*refs, has2):
    if has2:
        qn_ref, qp1_ref, kn1_ref, kp1_ref, v1_ref, qp2_ref, kn2_ref, kp2_ref, v2_ref, o_ref = refs
    else:
        qn_ref, qp1_ref, kn1_ref, kp1_ref, v1_ref, o_ref = refs
    scale = (QK_NOPE + QK_ROPE) ** -0.5
    lane = lax.broadcasted_iota(jnp.int32, qp1_ref.shape, 1)
    for hh in range(2):
        cols = slice(hh * QK_NOPE, (hh + 1) * QK_NOPE)
        mine = (lane < QK_ROPE) if hh == 0 else (lane >= QK_ROPE)
        qn = qn_ref[:, cols]

        def score(qp_ref, kn_ref, kp_ref):
            q = jnp.concatenate([qn, jnp.where(mine, qp_ref[...], 0.0)], axis=1).astype(BF16)
            k = jnp.concatenate([kn_ref[:, cols], kp_ref[...]], axis=1).astype(BF16)
            return _dot_nt(q, k) * scale

        scores = [score(qp1_ref, kn1_ref, kp1_ref)]
        values = [v1_ref[:, cols].astype(BF16)]
        if has2:
            scores.append(score(qp2_ref, kn2_ref, kp2_ref))
            values.append(v2_ref[:, cols].astype(BF16))
        o_ref[:, cols] = _softmax_read(scores, values, None).astype(o_ref.dtype)


def _mla_attn(qn, qp1, kn1, kp1, v1, seg2=None, *, nb, tq_len, out_rows, tq=256):
    pw = 2 * QK_NOPE
    qspec = lambda t, w: _tok_spec(tq, w, t[1], tq_len, t[2], True)
    kspec = lambda t, w, tk: _tok_spec(tk, w, t[1], tk, t[2], False)
    kpspec = lambda t, tk: pl.BlockSpec((tk, LANES), lambda b, h, qi: (t[1] // tk + b, 0))
    in_specs = [qspec(qn, pw), qspec(qp1, LANES), kspec(kn1, pw, tq_len), kpspec(kp1, tq_len),
                kspec(v1, pw, tq_len)]
    args = [qn[0], qp1[0], kn1[0], kp1[0], v1[0]]
    if seg2 is not None:
        qp2, kn2, kp2, v2, tk2 = seg2
        in_specs += [qspec(qp2, LANES), kspec(kn2, pw, tk2), kpspec(kp2, tk2), kspec(v2, pw, tk2)]
        args += [qp2[0], kn2[0], kp2[0], v2[0]]
    return pl.pallas_call(
        functools.partial(_mla_kernel, has2=seg2 is not None),
        grid=(nb, C_HEADS // 2, tq_len // tq),
        in_specs=in_specs,
        out_specs=_tok_spec(tq, pw, 0, tq_len, 0, True),
        out_shape=jax.ShapeDtypeStruct((out_rows, C_OUT_W), BF16),
        compiler_params=_cparams("parallel", "parallel", "arbitrary"),
        name="mla_attn",
    )(*args)


def _bdot(a, b, dims=_NN):
    return lax.dot_general(a.astype(BF16), b.astype(BF16), dims, preferred_element_type=F32)


def _bf16_parts(x, n):
    parts = []
    for _ in range(n - 1):
        hi = x.astype(BF16)
        parts.append(hi)
        x = x - hi.astype(F32)
    parts.append(x.astype(BF16))
    return parts


def _rwkv_kernel(r_ref, lw_ref, k_ref, v_ref, a_ref, b_ref, s0_ref, y_ref, s_ref,
                 p_scr, q_scr, rw_scr, y0_scr, *, seq, pairs, reverse, unroll):
    c = RWKV_CHUNK
    nchunk = seq // c
    gl = 2 * RWKV_PAIR
    hg = gl // D_HEAD_SIZE
    cs = hg * c
    width = pairs * RWKV_PAIR
    row = lax.broadcasted_iota(jnp.int32, (c, c), 0)
    col = lax.broadcasted_iota(jnp.int32, (c, c), 1)
    tri = ((col >= row) if reverse else (col <= row)).astype(BF16)
    last = 0 if reverse else c - 1
    head_of_lane = lax.broadcasted_iota(jnp.int32, (c, gl), 1) // D_HEAD_SIZE
    grow = lax.broadcasted_iota(jnp.int32, (cs, 2 * cs), 0)
    gcol = lax.broadcasted_iota(jnp.int32, (cs, 2 * cs), 1)
    gcol_in = jnp.where(gcol >= cs, gcol - cs, gcol)
    before = (gcol_in > grow) if reverse else (gcol_in < grow)
    upto = (gcol_in >= grow) if reverse else (gcol_in <= grow)
    eye = (lax.broadcasted_iota(jnp.int32, (cs, cs), 0) ==
           lax.broadcasted_iota(jnp.int32, (cs, cs), 1)).astype(F32)
    eye_pair = (lax.broadcasted_iota(jnp.int32, (RWKV_PAIR, RWKV_PAIR), 0) ==
                lax.broadcasted_iota(jnp.int32, (RWKV_PAIR, RWKV_PAIR), 1))
    n_double = c.bit_length() - 2

    def stacked(x):
        return jnp.concatenate([jnp.where(head_of_lane == h, x, 0.0) for h in range(hg)], axis=0)

    def local(step, carry):
        items = []
        for ui in range(unroll):
            cidx = step * unroll + ui
            rows = pl.ds(pl.multiple_of(cidx * c, c), c)
            r, lw, k = r_ref[0, rows, :], lw_ref[0, rows, :], k_ref[0, rows, :]
            v, a, b = v_ref[0, rows, :], a_ref[0, rows, :], b_ref[0, rows, :]
            cum = jnp.dot(tri, jnp.concatenate(_bf16_parts(lw, 3), axis=1), preferred_element_type=F32)
            lg = cum[:, :width] + (cum[:, width:2 * width] + cum[:, 2 * width:])
            lg_end = lg[last:last + 1]
            g_inv = jnp.exp(-lg)
            g_end = jnp.exp(lg_end - lg)
            g_chunk = jnp.exp(lg_end)
            full = (a * jnp.exp(lg - lw), r * jnp.exp(lg), b * g_inv, k * g_inv, v, b * g_end, k * g_end)
            for g in range(pairs // 2):
                ops = tuple(stacked(x[:, g * gl:(g + 1) * gl]) for x in full)
                items.append((cidx, g, g_chunk) + ops)
        n = range(len(items))
        at, rt, bt, kt, vs, bh, kh = ([it[3 + j] for it in items] for j in range(7))
        gram = [_bdot(jnp.concatenate([at[i], rt[i]], axis=0), jnp.concatenate([bt[i], kt[i]], axis=0), _NT)
                for i in n]
        g_u = [jnp.where(before, gram[i][:cs], 0.0) for i in n]
        g_y = [jnp.where(upto, gram[i][cs:], 0.0) for i in n]
        low = [g_u[i][:, :cs] for i in n]
        tinv = [eye + low[i] for i in n]
        pw = [_bdot(low[i], low[i]) for i in n]
        c0 = [_bdot(g_u[i][:, cs:], vs[i]) for i in n]
        for it in range(n_double):
            if it < n_double - 1:
                both = [_bdot(jnp.concatenate([pw[i], tinv[i]], axis=0), pw[i]) for i in n]
                pw = [both[i][:cs] for i in n]
                tinv = [tinv[i] + both[i][cs:] for i in n]
            else:
                tinv = [tinv[i] + _bdot(tinv[i], pw[i]) for i in n]
        wu = [_bdot(tinv[i], jnp.concatenate([at[i], c0[i]], axis=1)) for i in n]
        pmat = [_bdot(wu[i][:, :gl], bh[i], _TN) for i in n]
        qmat = [_bdot(jnp.concatenate([wu[i][:, gl:], vs[i]], axis=0),
                      jnp.concatenate([bh[i], kh[i]], axis=0), _TN) for i in n]
        ry = [_bdot(g_y[i], jnp.concatenate([wu[i], jnp.concatenate([jnp.zeros_like(vs[i]), vs[i]], axis=1)],
                                            axis=0)) for i in n]
        for i in n:
            cidx, g, g_chunk = items[i][:3]
            rw, y0 = rt[i] + ry[i][:, :gl], ry[i][:, gl:]
            for j in range(2):
                p = 2 * g + j
                lanes = slice(j * RWKV_PAIR, (j + 1) * RWKV_PAIR)
                head_rows = slice(2 * j * c, (2 * j + 2) * c)
                decay = g_chunk[:, p * RWKV_PAIR:(p + 1) * RWKV_PAIR]
                p_scr[cidx, p] = jnp.where(eye_pair, decay, 0.0) + pmat[i][lanes, lanes]
                q_scr[cidx, p] = qmat[i][lanes, lanes]
                rw_scr[cidx, p] = rw[head_rows, lanes]
                y0_scr[cidx, p] = y0[head_rows, lanes]
        return carry

    lax.fori_loop(0, nchunk // unroll, local, 0)
    s_ref[0] = s0_ref[0]

    def scan(ci, carry):
        cidx = (nchunk - 1 - ci) if reverse else ci
        rows = pl.ds(pl.multiple_of(cidx * c, c), c)
        s = [s_ref[0, p] for p in range(pairs)]
        ys = [_bdot(rw_scr[cidx, p], s[p], _NT) for p in range(pairs)]
        s_new = [_bdot(s[p], p_scr[cidx, p]) for p in range(pairs)]
        for p in range(pairs):
            y = ys[p] + y0_scr[cidx, p]
            y_ref[0, rows, p * RWKV_PAIR:(p + 1) * RWKV_PAIR] = y[:c] + y[c:]
            s_ref[0, p] = s_new[p] + q_scr[cidx, p]
        return carry

    lax.fori_loop(0, nchunk, scan, 0)


def _rwkv_scan(r, lw, k, v, a, b, s0, *, reverse, pairs, unroll):
    nb, seq, _ = r.shape
    assert 4 * RWKV_CHUNK == RWKV_PAIR and seq % (RWKV_CHUNK * unroll) == 0
    assert pairs % 2 == 0 and N_PAIRS % pairs == 0
    nchunk = seq // RWKV_CHUNK
    tok = pl.BlockSpec((1, seq, pairs * RWKV_PAIR), lambda bi, pi: (bi, 0, pi))
    st = pl.BlockSpec((1, pairs, RWKV_PAIR, RWKV_PAIR), lambda bi, pi: (bi, pi, 0, 0))
    pair_mat = pltpu.VMEM((nchunk, pairs, RWKV_PAIR, RWKV_PAIR), F32)
    pair_rows = pltpu.VMEM((nchunk, pairs, 2 * RWKV_CHUNK, RWKV_PAIR), F32)
    return pl.pallas_call(
        functools.partial(_rwkv_kernel, seq=seq, pairs=pairs, reverse=reverse, unroll=unroll),
        grid=(nb, N_PAIRS // pairs),
        in_specs=[tok] * 6 + [st],
        out_specs=[tok, st],
        out_shape=[jax.ShapeDtypeStruct(r.shape, F32), jax.ShapeDtypeStruct(s0.shape, F32)],
        scratch_shapes=[pair_mat, pair_mat, pair_rows, pair_rows],
        compiler_params=_cparams("parallel", "parallel"),
        name="rwkv_rev" if reverse else "rwkv_fwd",
    )(r, lw, k, v, a, b, s0)


def _moe_kernel(be_ref, nv_ref, cnt_ref, idx_ref, w_ref, h_hbm, wg_ref, wu_ref, wd_ref, y_hbm,
                xbuf, gacc, uacc, hbuf, acc, gsem, ssem):
    i = pl.program_id(0)
    f = pl.program_id(1)
    n_used = nv_ref[0]
    cur = i % 2

    def row_in(blk, r, dst):
        tok = idx_ref[blk * MOE_ROWS + r] & MOE_TOK_MASK
        return pltpu.make_async_copy(h_hbm.at[pl.ds(tok, 1)], xbuf.at[dst, pl.ds(r, 1)], gsem.at[dst])

    def row_out(r):
        slot = idx_ref[i * MOE_ROWS + r] >> MOE_TOK_BITS
        return pltpu.make_async_copy(acc.at[pl.ds(r, 1)], y_hbm.at[pl.ds(slot, 1)], ssem)

    def groups(blk):
        return (cnt_ref[blk] + MOE_DMA_GROUP - 1) // MOE_DMA_GROUP

    def gather(blk, dst):
        def body(g, carry):
            for r in range(MOE_DMA_GROUP):
                row_in(blk, g * MOE_DMA_GROUP + r, dst).start()
            return carry
        lax.fori_loop(0, groups(blk), body, 0)

    @pl.when((f == 0) & (i == 0))
    def _():
        xbuf[...] = jnp.zeros_like(xbuf)

    @pl.when((f == 0) & (i == 0) & (n_used > 0))
    def _():
        gather(0, 0)

    @pl.when((f == 0) & (i < n_used))
    def _():
        n = pl.multiple_of(groups(i) * MOE_DMA_GROUP, MOE_DMA_GROUP)
        pltpu.make_async_copy(h_hbm.at[pl.ds(0, n)], xbuf.at[cur, pl.ds(0, n)], gsem.at[cur]).wait()

        @pl.when(i + 1 < n_used)
        def _():
            gather(i + 1, 1 - cur)

    half_d = xbuf.shape[2] // 2
    half_ff = hbuf.shape[1] // 2

    def gate_up(lo):
        x = xbuf[cur, :, lo:lo + half_d].astype(BF16)
        return (jnp.dot(x, wg_ref[...].astype(BF16), preferred_element_type=F32),
                jnp.dot(x, wu_ref[...].astype(BF16), preferred_element_type=F32))

    def down(lo):
        return jnp.dot(hbuf[:, lo:lo + half_ff], wd_ref[...].astype(BF16), preferred_element_type=F32)

    for parity, (d_lo, d_hi, ff_lo, ff_hi) in enumerate(((0, half_d, 0, half_ff), (half_d, 0, half_ff, 0))):
        mine = (i < n_used) & (i % 2 == parity)

        @pl.when(mine & (f == 0))
        def _():
            gacc[...], uacc[...] = gate_up(d_lo)

        @pl.when(mine & (f == 1))
        def _():
            g, u = gate_up(d_hi)
            g, u = gacc[...] + g, uacc[...] + u
            hbuf[...] = ((g * jax.nn.sigmoid(g)) * u).astype(BF16)
            acc[...] = down(ff_lo)

        @pl.when(mine & (f == 2))
        def _():
            acc[...] = (acc[...] + down(ff_hi)) * w_ref[...]

    @pl.when((i < n_used) & (f == 2))
    def _():
        n = cnt_ref[i]
        whole = n // MOE_DMA_GROUP

        def body(g, carry):
            for r in range(MOE_DMA_GROUP):
                row_out(g * MOE_DMA_GROUP + r).start()
            return carry
        lax.fori_loop(0, whole, body, 0)

        def rest(r, carry):
            row_out(r).start()
            return carry
        lax.fori_loop(whole * MOE_DMA_GROUP, n, rest, 0)

        @pl.when(whole > 0)
        def _():
            m = pl.multiple_of(whole * MOE_DMA_GROUP, MOE_DMA_GROUP)
            pltpu.make_async_copy(acc.at[pl.ds(0, m)], y_hbm.at[pl.ds(0, m)], ssem).wait()

        def rest_wait(r, carry):
            row_out(r).wait()
            return carry
        lax.fori_loop(whole * MOE_DMA_GROUP, n, rest_wait, 0)


def _moe_experts(layer, block_e, n_used, row_cnt, row_idx, row_w, h, w_gate, w_up, w_down):
    n_blocks = block_e.shape[0]
    n_tok, d = h.shape
    ff = w_gate.shape[-1]
    flip = lambda i, half: jnp.where(i % 2 == 0, half, 1 - half)
    in_half = pl.BlockSpec((None, None, d // 2, ff),
                           lambda i, f, be, nv, ct, ix: (layer, be[i], flip(i, jnp.minimum(f, 1)), 0))
    return pl.pallas_call(
        _moe_kernel,
        grid_spec=pltpu.PrefetchScalarGridSpec(
            num_scalar_prefetch=4,
            grid=(n_blocks, 3),
            in_specs=[pl.BlockSpec((MOE_ROWS, 1), lambda i, f, be, nv, ct, ix: (i, 0)),
                      pl.BlockSpec(memory_space=pl.ANY),
                      in_half, in_half,
                      pl.BlockSpec((None, None, ff // 2, d),
                                   lambda i, f, be, nv, ct, ix: (layer, be[i], flip(i, jnp.maximum(f - 1, 0)), 0))],
            out_specs=pl.BlockSpec(memory_space=pl.ANY),
            scratch_shapes=[pltpu.VMEM((2, MOE_ROWS, d), F32), pltpu.VMEM((MOE_ROWS, ff), F32),
                            pltpu.VMEM((MOE_ROWS, ff), F32), pltpu.VMEM((MOE_ROWS, ff), BF16),
                            pltpu.VMEM((MOE_ROWS, d), F32),
                            pltpu.SemaphoreType.DMA((2,)), pltpu.SemaphoreType.DMA(())]),
        out_shape=jax.ShapeDtypeStruct((n_tok * TOP_K, d), F32),
        compiler_params=_cparams("arbitrary", "arbitrary"),
        name="moe_experts",
    )(block_e, n_used, row_cnt, row_idx, row_w, h, w_gate, w_up, w_down)


def _moe(layer, h_f32, logits, b_router, w_gate, w_up, w_down):
    n_tok = h_f32.shape[0]
    logits = logits[:, :N_GROUPS + N_EXPERTS] + b_router
    gp = jax.nn.softmax(logits[:, :N_GROUPS], axis=-1)
    g_idx = jnp.argmax(gp, axis=-1)[:, None]
    p_g = jnp.max(gp, axis=-1, keepdims=True)
    el = logits[:, N_GROUPS:].reshape(-1, N_GROUPS, EXPERTS_PER_GROUP)
    el = jnp.take_along_axis(el, g_idx[:, :, None], axis=1)[:, 0]
    pe = jax.nn.softmax(el, axis=-1)
    e_first = jnp.argmax(pe, axis=-1)[:, None]
    pe_rest = jnp.where(jnp.arange(EXPERTS_PER_GROUP)[None, :] == e_first, -jnp.inf, pe)
    e_loc = jnp.concatenate([e_first, jnp.argmax(pe_rest, axis=-1)[:, None]], axis=1)
    p_e = jnp.concatenate([jnp.max(pe, axis=-1, keepdims=True), jnp.max(pe_rest, axis=-1, keepdims=True)], axis=1)
    wts = p_g * p_e / jnp.sum(p_e, axis=-1, keepdims=True)
    idx = g_idx * EXPERTS_PER_GROUP + e_loc

    n_slots = n_tok * TOP_K
    assert n_tok <= MOE_TOK_MASK and n_slots % MOE_ROWS == 0
    flat_e = idx.reshape(-1).astype(jnp.int32)
    order = jnp.argsort(flat_e).astype(jnp.int32)
    counts = jnp.bincount(flat_e, length=N_EXPERTS)
    padded = (counts + MOE_ROWS - 1) // MOE_ROWS * MOE_ROWS
    pad_end = jnp.cumsum(padded)
    pad_start = pad_end - padded
    start = jnp.cumsum(counts) - counts
    n_blocks = n_slots // MOE_ROWS + N_EXPERTS
    n_rows = n_blocks * MOE_ROWS
    blk_start = jnp.arange(n_blocks) * MOE_ROWS
    block_e = jnp.minimum(jnp.searchsorted(pad_end, blk_start, side='right'), N_EXPERTS - 1).astype(jnp.int32)
    row_cnt = jnp.clip(counts[block_e] - (blk_start - pad_start[block_e]), 0, MOE_ROWS).astype(jnp.int32)
    row_e = jnp.repeat(block_e, MOE_ROWS)
    within = jnp.arange(n_rows) - pad_start[row_e]
    real = (within >= 0) & (within < counts[row_e])
    flat = order[jnp.clip(start[row_e] + within, 0, n_slots - 1)]
    tok = flat // TOP_K
    slot = (flat % TOP_K) * n_tok + tok
    row_idx = jnp.where(real, tok | (slot << MOE_TOK_BITS), 0).astype(jnp.int32)
    row_w = jnp.where(real, wts.reshape(-1)[flat], 0.0)
    n_used = (pad_end[-1] // MOE_ROWS).astype(jnp.int32).reshape(1)
    return _moe_experts(layer, block_e, n_used, row_cnt, row_idx, row_w.reshape(n_rows, 1), h_f32,
                        w_gate, w_up, w_down)


def _axial_rope_tables(n_tok, rot_dim):
    t = jnp.arange(n_tok)
    rows = (t // GRID_W).astype(F32)
    cols = (t % GRID_W).astype(F32)
    axis_dim = rot_dim // 2
    inv = ROPE_BASE ** (-jnp.arange(0, axis_dim, 2, dtype=F32) / axis_dim)
    ang_r = rows[:, None] * inv[None, :]
    ang_c = cols[:, None] * inv[None, :]
    return jnp.cos(ang_r), jnp.sin(ang_r), jnp.cos(ang_c), jnp.sin(ang_c)


def _rope_rotate(x, cos, sin):
    half = x.shape[-1] // 2
    c, s = cos[None, :, None, :], sin[None, :, None, :]
    x1, x2 = x[..., :half], x[..., half:]
    return jnp.concatenate([x1 * c - x2 * s, x2 * c + x1 * s], axis=-1)


def _apply_axial_rope(x, tables):
    cos_r, sin_r, cos_c, sin_c = tables
    h = x.shape[-1] // 2
    return jnp.concatenate([_rope_rotate(x[..., :h], cos_r, sin_r),
                            _rope_rotate(x[..., h:], cos_c, sin_c)], axis=-1)


def _conv_kernel(b_ref, c_ref, u_ref, cp_ref, up_ref, cn_ref, un_ref, w_ref, o_ref, *, tm):
    i = pl.program_id(0)
    nc = N_CTX_TOK // tm
    in_seq = jnp.where(i < nc, i % (SEQ // tm), (i - nc) % (DEC_SEQ // tm))
    last_in_seq = jnp.where(i < nc, SEQ // tm - 1, DEC_SEQ // tm - 1)
    z = c_ref[...] * u_ref[...]
    z_before = jnp.where(in_seq == 0, 0.0, cp_ref[7:8, :] * up_ref[7:8, :])
    z_after = jnp.where(in_seq == last_in_seq, 0.0, cn_ref[0:1, :] * un_ref[0:1, :])
    row = lax.broadcasted_iota(jnp.int32, (tm, 1), 0)
    z_prev = jnp.where(row == 0, z_before, pltpu.roll(z, 1, axis=0))
    z_next = jnp.where(row == tm - 1, z_after, pltpu.roll(z, tm - 1, axis=0))
    conv = w_ref[0:1, :] * z_prev + w_ref[1:2, :] * z + w_ref[2:3, :] * z_next
    o_ref[...] = (b_ref[...] * conv).astype(o_ref.dtype)


def _short_conv(p, col0, conv_w, tm=128, tw=1024):
    assert col0 % tw == 0 and B_WIDTH % tw == 0 and SEQ % tm == 0 and tm % 8 == 0
    c0, nw, g8, last8 = col0 // tw, B_WIDTH // tw, tm // 8, p.shape[0] // 8 - 1
    tile = lambda n: pl.BlockSpec((tm, tw), lambda i, c: (i, c0 + n * nw + c))
    before = lambda n: pl.BlockSpec((8, tw), lambda i, c: (jnp.maximum(i * g8 - 1, 0), c0 + n * nw + c))
    after = lambda n: pl.BlockSpec((8, tw), lambda i, c: (jnp.minimum((i + 1) * g8, last8), c0 + n * nw + c))
    return pl.pallas_call(
        functools.partial(_conv_kernel, tm=tm),
        grid=(p.shape[0] // tm, nw),
        in_specs=[tile(0), tile(1), tile(2), before(1), before(2), after(1), after(2),
                  pl.BlockSpec((3, tw), lambda i, c: (0, c))],
        out_specs=pl.BlockSpec((tm, tw), lambda i, c: (i, c)),
        out_shape=jax.ShapeDtypeStruct((p.shape[0], B_WIDTH), BF16),
        compiler_params=_cparams("parallel", "parallel"),
        name="short_conv",
    )(p, p, p, p, p, p, p, conv_w)


def _out_proj_kernel(a1_ref, a2_ref, w_ref, res_ref, gate_ref, o_ref):
    k1 = a1_ref.shape[1]
    w = w_ref[...].astype(BF16)
    acc = (jnp.dot(a1_ref[...], w[:k1], preferred_element_type=F32) +
           jnp.dot(a2_ref[...], w[k1:], preferred_element_type=F32))
    o_ref[...] = res_ref[...] + gate_ref[0] * acc


def _out_proj(a1, a2, w, w_idx, res, gate, tm=1024, tn=512, name="out_proj"):
    m, k1 = a1.shape
    k2 = a2.shape[1]
    n = w.shape[-1]
    assert w.shape[-2] == k1 + k2 and m % tm == 0 and n % tn == 0
    lead = tuple(w_idx)
    return pl.pallas_call(
        _out_proj_kernel,
        grid=(m // tm, n // tn),
        in_specs=[pl.BlockSpec((tm, k1), lambda i, j: (i, 0)), pl.BlockSpec((tm, k2), lambda i, j: (i, 0)),
                  pl.BlockSpec((None,) * len(lead) + (k1 + k2, tn), lambda i, j: lead + (0, j)),
                  pl.BlockSpec((tm, tn), lambda i, j: (i, j)),
                  pl.BlockSpec((1, 1, tn), lambda i, j: (_row_group(i, tm), 0, j))],
        out_specs=pl.BlockSpec((tm, tn), lambda i, j: (i, j)),
        out_shape=jax.ShapeDtypeStruct((m, n), F32),
        compiler_params=_cparams("parallel", "parallel"),
        name=name,
    )(a1, a2, w, res, gate.reshape(N_MOD_GROUPS, 1, n))


def _even_mixer(h, i, cache_k, cache_v, w_in, w_out, sink, conv_w, x, gate):
    p = _mm(h, w_in, (i,), name="even_in")
    o_k, o_v, o_b = A_Q_W, A_Q_W + A_KV_W, A_Q_W + 2 * A_KV_W
    att_c = _gqa_attn(sink[i], (p, 0, 0), (p, 0, o_k), (p, 0, o_v),
                      nb=BATCH, tq_len=SEQ, band=False, out_rows=N_CTX_TOK, tq=SEQ)
    pl_ = p[N_CTX_TOK:]
    tabs = _axial_rope_tables(DEC_SEQ, HEAD_DIM)
    q_rot = _apply_axial_rope(pl_[:, :A_Q_W].reshape(DEC_BATCH, DEC_SEQ, A_HEADS, HEAD_DIM), tabs)
    k_rot = _apply_axial_rope(pl_[:, o_k:o_v].reshape(DEC_BATCH, DEC_SEQ, A_KV_HEADS, HEAD_DIM), tabs)
    ck = cache_k[:, i].reshape(DEC_BATCH * PAST_LEN, A_KV_W)
    cv = cache_v[:, i].reshape(DEC_BATCH * PAST_LEN, A_KV_W)
    att_l = _gqa_attn(sink[i], (q_rot.reshape(N_LAT_TOK, A_Q_W), 0, 0),
                      (k_rot.reshape(N_LAT_TOK, A_KV_W), 0, 0), (p, N_CTX_TOK, o_v),
                      ((p, N_CTX_TOK, 0), (ck, 0, 0), (cv, 0, 0), PAST_LEN),
                      nb=DEC_BATCH, tq_len=DEC_SEQ, band=True, out_rows=N_LAT_TOK)
    conv = _short_conv(p, o_b, conv_w[i])
    x = _out_proj(jnp.concatenate([att_c, att_l], axis=0), conv, w_out, (i,), x, gate, name="even_out")
    new_k = p[:N_CTX_TOK, o_k:o_v].reshape(BATCH, SEQ, A_KV_HEADS, HEAD_DIM)
    new_v = p[:N_CTX_TOK, o_v:o_b].reshape(BATCH, SEQ, A_KV_HEADS, HEAD_DIM)
    return x, new_k, new_v


def _rms(x, g):
    return x * lax.rsqrt(jnp.mean(x * x, axis=-1, keepdims=True) + NORM_EPS) * g


def _head_sums(x):
    rows, width = x.shape
    nt = width // LANES
    ri = lax.broadcasted_iota(jnp.int32, (LANES, LANES), 0) // D_HEAD_SIZE
    ci = lax.broadcasted_iota(jnp.int32, (LANES, LANES), 1) // D_HEAD_SIZE
    same_head = (ri == ci).astype(BF16)
    stacked = jnp.concatenate([part[:, j * LANES:(j + 1) * LANES] for part in _bf16_parts(x, 3) for j in range(nt)],
                              axis=0)
    s = jnp.dot(stacked, same_head, preferred_element_type=F32)
    n = nt * rows
    s = s[:n] + (s[n:2 * n] + s[2 * n:])
    return jnp.concatenate([s[j * rows:(j + 1) * rows] for j in range(nt)], axis=1)


def _rwkv_prep_kernel(x_ref, edge_ref, mu_ref, w0_ref, a0_ref, kk_ref, ka_ref, wup_ref, aup_ref,
                      r_ref, lw_ref, k_ref, v_ref, a_ref, b_ref, *, tm, seq, reverse):
    j = pl.program_id(0)
    blocks_per_seq = seq // tm
    x = x_ref[...]
    row = lax.broadcasted_iota(jnp.int32, (tm, 1), 0)
    if reverse:
        edge = jnp.where(j % blocks_per_seq == blocks_per_seq - 1, 0.0, edge_ref[0:1, :])
        nbr = jnp.where(row == tm - 1, edge, pltpu.roll(x, tm - 1, axis=0))
    else:
        edge = jnp.where(j % blocks_per_seq == 0, 0.0, edge_ref[7:8, :])
        nbr = jnp.where(row == 0, edge, pltpu.roll(x, 1, axis=0))
    z = x + mu_ref[...] * (nbr - x)
    r, k, v = (z[:, n * D_WIDTH:(n + 1) * D_WIDTH] for n in range(3))
    tail = z[:, 3 * D_WIDTH:]
    w_raw = w0_ref[...] + _bdot(jnp.tanh(tail), wup_ref[...])
    sp = jnp.maximum(-w_raw, 0.0) + jnp.log(1.0 + jnp.exp(-jnp.abs(w_raw)))
    lw_ref[...] = -jnp.exp(-sp - 0.5)
    a = jax.nn.sigmoid(a0_ref[...] + _bdot(tail, aup_ref[...]))
    kk = k * kk_ref[...]
    kk = kk / jnp.maximum(jnp.sqrt(_head_sums(kk * kk)), 1e-12)
    r_ref[...] = r
    k_ref[...] = k * (1 + (a - 1) * ka_ref[...])
    v_ref[...] = v
    a_ref[...] = -kk
    b_ref[...] = kk * a


def _rwkv_prep(rw, row0, nb, seq, mu, w0, a0, k_k, k_a, wup_pad, aup_pad, *, reverse, tm=128):
    rows = nb * seq
    assert row0 % tm == 0 and seq % tm == 0 and tm % 8 == 0
    b0, g8 = row0 // tm, tm // 8
    if reverse:
        edge = pl.BlockSpec((8, SHIFT_W), lambda j: (jnp.minimum((b0 + j + 1) * g8, rw.shape[0] // 8 - 1), 0))
    else:
        edge = pl.BlockSpec((8, SHIFT_W), lambda j: (jnp.maximum((b0 + j) * g8 - 1, 0), 0))
    vec = lambda n: pl.BlockSpec((1, n), lambda j: (0, 0))
    lora = pl.BlockSpec((LANES, D_WIDTH), lambda j: (0, 0))
    out = pl.BlockSpec((tm, D_WIDTH), lambda j: (j, 0))
    return pl.pallas_call(
        functools.partial(_rwkv_prep_kernel, tm=tm, seq=seq, reverse=reverse),
        grid=(rows // tm,),
        in_specs=[pl.BlockSpec((tm, SHIFT_W), lambda j: (b0 + j, 0)), edge, vec(SHIFT_W)] + [vec(D_WIDTH)] * 4 +
                 [lora, lora],
        out_specs=[out] * 6,
        out_shape=[jax.ShapeDtypeStruct((rows, D_WIDTH), F32)] * 6,
        compiler_params=_cparams("parallel"),
        name="rwkv_prep",
    )(rw, rw, mu.reshape(1, -1), w0.reshape(1, -1), a0.reshape(1, -1), k_k.reshape(1, -1), k_a.reshape(1, -1),
      wup_pad, aup_pad)


def _rwkv_readout_kernel(*refs):
    dirs = (refs[0:4], refs[4:8])
    glo_ref, gup_ref, rk_ref, lnw_ref, lnb_ref, o_ref = refs[8:]
    total = None
    for d, (y_ref, r_ref, k_ref, v_ref) in enumerate(dirs):
        y = y_ref[...]
        dev = y - _head_sums(y) * (1.0 / D_HEAD_SIZE)
        var = _head_sums(dev * dev) * (1.0 / D_HEAD_SIZE)
        yn = dev * lax.rsqrt(var + GN_EPS) * lnw_ref[...] + lnb_ref[...]
        out = yn + _head_sums(r_ref[...] * k_ref[...] * rk_ref[d:d + 1, :]) * v_ref[...]
        total = out if total is None else total + out
    gate = _bdot(jax.nn.sigmoid(glo_ref[...]), gup_ref[...])
    o_ref[...] = (total * gate).astype(o_ref.dtype)


def _rwkv_readout(per_dir, g_lo, row0, g_up, r_k, ln_w, ln_b, tm=128):
    rows = per_dir[0][0].shape[0]
    tok = pl.BlockSpec((tm, D_WIDTH), lambda j: (j, 0))
    vec = pl.BlockSpec((1, D_WIDTH), lambda j: (0, 0))
    return pl.pallas_call(
        _rwkv_readout_kernel,
        grid=(rows // tm,),
        in_specs=[tok] * 8 + [pl.BlockSpec((tm, GATE_LORA), lambda j: (row0 // tm + j, 0)),
                              pl.BlockSpec((GATE_LORA, D_WIDTH), lambda j: (0, 0)),
                              pl.BlockSpec((2, D_WIDTH), lambda j: (0, 0)), vec, vec],
        out_specs=tok,
        out_shape=jax.ShapeDtypeStruct((rows, D_WIDTH), BF16),
        compiler_params=_cparams("parallel"),
        name="rwkv_readout",
    )(*per_dir[0], *per_dir[1], g_lo, g_up, r_k.reshape(2, D_WIDTH), ln_w.reshape(1, -1), ln_b.reshape(1, -1))


def _rwkv_group(rw, g_lo, row0, s0, i, rp, nb, seq, pairs):
    per_dir, finals = [], []
    zeros = jnp.zeros((LANES - DECAY_LORA, D_WIDTH), F32)
    for d in range(2):
        wup_pad = jnp.concatenate([rp['w_up'][i, d], zeros], axis=0)
        aup_pad = jnp.concatenate([zeros, rp['a_up'][i, d]], axis=0)
        r, lw, k, v, a, b = _rwkv_prep(rw, row0, nb, seq, rp['mu'][i, d], rp['w0'][i, d], rp['a0'][i, d],
                                       rp['k_k'][i, d], rp['k_a'][i, d], wup_pad, aup_pad, reverse=(d == 1))
        to3 = lambda t: t.reshape(nb, seq, D_WIDTH)
        sd = s0[:, d].reshape(nb, N_PAIRS, 2, D_HEAD_SIZE, D_HEAD_SIZE)
        eye2 = jnp.eye(2, dtype=F32)
        s_bd = jnp.einsum('bphij,hg->bphigj', sd, eye2).reshape(nb, N_PAIRS, RWKV_PAIR, RWKV_PAIR)
        y, s_fin = _rwkv_scan(to3(r), to3(lw), to3(k), to3(v), to3(a), to3(b), s_bd,
                              reverse=(d == 1), pairs=pairs, unroll=16 // pairs)
        s_fin = s_fin.reshape(nb, N_PAIRS, 2, D_HEAD_SIZE, 2, D_HEAD_SIZE)
        s_fin = jnp.stack([s_fin[:, :, 0, :, 0], s_fin[:, :, 1, :, 1]], axis=2)
        finals.append(s_fin.reshape(nb, D_HEADS, D_HEAD_SIZE, D_HEAD_SIZE))
        per_dir.append((y.reshape(nb * seq, D_WIDTH), r, k, v))
    out = _rwkv_readout(per_dir, g_lo, row0, rp['g_up'][i], rp['r_k'][i], rp['ln_w'][i], rp['ln_b'][i])
    return out, jnp.stack(finals, axis=1)


def _odd_mixer(h, i, cache_ckv, cache_kpe, state, w_in_t, w_out, mp, rp, x, gate):
    p = _mm(h, w_in_t, (i,), w_t=True, name="odd_in")
    o1 = Q_LORA
    o2 = o1 + KV_LORA
    o3 = o2 + QK_ROPE
    o4 = o3 + SHIFT_W
    q_down, kv_down, k_pe, rw, g_lo = p[:, :o1], p[:, o1:o2], p[:, o2:o3], p[:, o3:o4], p[:, o4:]
    w_uq = mp['w_uq'][i].reshape(Q_LORA, C_HEADS, QK_NOPE + QK_ROPE)
    w_uq = jnp.concatenate([w_uq[:, :, :QK_NOPE].reshape(Q_LORA, -1), w_uq[:, :, QK_NOPE:].reshape(Q_LORA, -1)],
                           axis=1)
    w_ukv = mp['w_ukv'][i].reshape(KV_LORA, C_HEADS, QK_NOPE + V_DIM)
    w_ukv = jnp.concatenate([w_ukv[:, :, :QK_NOPE].reshape(KV_LORA, -1), w_ukv[:, :, QK_NOPE:].reshape(KV_LORA, -1)],
                            axis=1)
    q = _mm(_rms(q_down, mp['q_norm'][i]), w_uq, name="mla_uq")
    c_kv = _rms(kv_down, mp['kv_norm'][i])
    kv = _mm(c_kv, w_ukv, name="mla_ukv")
    kv_ctx = _mm(cache_ckv[:, i].reshape(DEC_BATCH * PAST_LEN, KV_LORA), w_ukv, name="mla_ukv_cache")
    n_nope = C_HEADS * QK_NOPE
    dup = lambda t: jnp.concatenate([t, t], axis=-1)
    kp = dup(k_pe)
    att_c = _mla_attn((q, 0, 0), (q, 0, n_nope), (kv, 0, 0), (kp, 0, 0), (kv, 0, n_nope),
                      nb=BATCH, tq_len=SEQ, out_rows=N_CTX_TOK)
    tabs = _axial_rope_tables(DEC_SEQ, QK_ROPE)
    qpe_rot = _apply_axial_rope(q[N_CTX_TOK:, n_nope:].reshape(DEC_BATCH, DEC_SEQ, C_HEADS, QK_ROPE), tabs)
    kpe_rot = _apply_axial_rope(k_pe[N_CTX_TOK:].reshape(DEC_BATCH, DEC_SEQ, 1, QK_ROPE), tabs)
    kp_ctx = dup(cache_kpe[:, i].reshape(DEC_BATCH * PAST_LEN, QK_ROPE))
    att_l = _mla_attn((q, N_CTX_TOK, 0), (qpe_rot.reshape(N_LAT_TOK, -1), 0, 0), (kv, N_CTX_TOK, 0),
                      (dup(kpe_rot.reshape(N_LAT_TOK, QK_ROPE)), 0, 0), (kv, N_CTX_TOK, n_nope),
                      ((q, N_CTX_TOK, n_nope), (kv_ctx, 0, 0), (kp_ctx, 0, 0), (kv_ctx, 0, n_nope), PAST_LEN),
                      nb=DEC_BATCH, tq_len=DEC_SEQ, out_rows=N_LAT_TOK)
    s_zero = jnp.zeros((BATCH, 2, D_HEADS, D_HEAD_SIZE, D_HEAD_SIZE), F32)
    rw_c, s_c = _rwkv_group(rw, g_lo, 0, s_zero, i, rp, BATCH, SEQ, 8)
    rw_l, _ = _rwkv_group(rw, g_lo, N_CTX_TOK, state[:, i].astype(F32), i, rp, DEC_BATCH, DEC_SEQ, 2)
    x = _out_proj(jnp.concatenate([att_c, att_l], axis=0), jnp.concatenate([rw_c, rw_l], axis=0), w_out, (i,),
                  x, gate, name="odd_out")
    new_ckv = c_kv[:N_CTX_TOK].reshape(BATCH, SEQ, KV_LORA)
    new_kpe = k_pe[:N_CTX_TOK].reshape(BATCH, SEQ, QK_ROPE)
    return x, new_ckv, new_kpe, s_c


def kernel(x_prompt, x_sample, c, c_ctx, cache_attn_k, cache_attn_v, cache_mla_ckv, cache_mla_kpe,
           state_rwkv, w_mod, b_mod, norm_mix, norm_ffn, norm_final, even_w_in, even_w_out, attn_sink,
           conv_w, odd_w_in, odd_w_out, mla_q_norm, mla_kv_norm, mla_w_uq, mla_w_ukv, rwkv_mu, rwkv_w0,
           rwkv_w_up, rwkv_a0, rwkv_a_up, rwkv_k_k, rwkv_k_a, rwkv_r_k, rwkv_g_up, rwkv_ln_w, rwkv_ln_b,
           router_group_w, router_group_b, router_expert_w, router_expert_b, expert_w_gate, expert_w_up,
           expert_w_down):
    d = D_MODEL
    x = jnp.concatenate([x_prompt.reshape(N_CTX_TOK, d), x_sample.reshape(N_LAT_TOK, d)], axis=0)
    cvec = jax.nn.silu(jnp.concatenate([c_ctx[None], c], axis=0))
    cvec = jnp.pad(cvec, ((0, 8 - N_MOD_GROUPS), (0, 0)))
    mp = dict(q_norm=mla_q_norm, kv_norm=mla_kv_norm, w_uq=mla_w_uq, w_ukv=mla_w_ukv)
    rp = dict(mu=rwkv_mu, w0=rwkv_w0, w_up=rwkv_w_up, a0=rwkv_a0, a_up=rwkv_a_up, k_k=rwkv_k_k,
              k_a=rwkv_k_a, r_k=rwkv_r_k, g_up=rwkv_g_up, ln_w=rwkv_ln_w, ln_b=rwkv_ln_b)
    odd_w_in_t = jnp.swapaxes(odd_w_in, 1, 2)
    new_k, new_v, new_ckv, new_kpe, new_s = [], [], [], [], []
    ffn = None
    for l in range(DEPTH):
        mod = _mm(cvec, w_mod, (l,), tn=1024, tk=2048, name="modulation")[:N_MOD_GROUPS] + b_mod[l]
        shift_mix, scale_mix, gate_mix, shift_ffn, scale_ffn, gate_ffn = jnp.split(mod, 6, axis=-1)
        if ffn is None:
            (h,) = _normmod(x, norm_mix[l], shift_mix, scale_mix, (BF16,))
        else:
            x, h = _normmod(x, norm_mix[l], shift_mix, scale_mix, (BF16,), res=ffn, emit_x=True)
        i = l // 2
        if l % 2 == 0:
            x, k_c, v_c = _even_mixer(h, i, cache_attn_k, cache_attn_v, even_w_in, even_w_out, attn_sink,
                                      conv_w, x, gate_mix)
            new_k.append(k_c)
            new_v.append(v_c)
        else:
            x, ckv_c, kpe_c, s_c = _odd_mixer(h, i, cache_mla_ckv, cache_mla_kpe, state_rwkv, odd_w_in_t,
                                              odd_w_out, mp, rp, x, gate_mix)
            new_ckv.append(ckv_c)
            new_kpe.append(kpe_c)
            new_s.append(s_c)
        w_router = jnp.pad(jnp.concatenate([router_group_w[l], router_expert_w[l]], axis=1),
                           ((0, 0), (0, LANES - N_GROUPS - N_EXPERTS)))
        b_router = jnp.concatenate([router_group_b[l], router_expert_b[l]])
        h_ffn, logits = _normmod(x, norm_ffn[l], shift_ffn, scale_ffn, (F32,), router=w_router)
        ffn = (_moe(l, h_ffn, logits, b_router, expert_w_gate, expert_w_up, expert_w_down), gate_ffn)
    zeros = jnp.zeros((N_MOD_GROUPS, d), F32)
    (y_all,) = _normmod(x, norm_final, zeros, zeros, (F32,), res=ffn)
    return (y_all[:N_CTX_TOK].reshape(BATCH, SEQ, d), y_all[N_CTX_TOK:].reshape(DEC_BATCH, DEC_SEQ, d),
            jnp.stack(new_k, axis=1), jnp.stack(new_v, axis=1), jnp.stack(new_ckv, axis=1),
            jnp.stack(new_kpe, axis=1), jnp.stack(new_s, axis=1))
```

```python
import functools

import jax
import jax.numpy as jnp
from jax import lax
from jax.experimental import pallas as pl
from jax.experimental.pallas import tpu as pltpu

F32 = jnp.float32
BF16 = jnp.bfloat16

D_MODEL = 4096
BATCH = 16
SEQ = 256
DEPTH = 4
DEC_BATCH = 2
DEC_SEQ = 1024
PAST_LEN = 512
GRID_W = 64
WINDOW = 128
ROPE_BASE = 10000.0
NORM_EPS = 1e-6
NEG_INF = -1e30

HEAD_DIM = 128
A_HEADS = 16
A_KV_HEADS = 4
A_GROUP = A_HEADS // A_KV_HEADS
A_Q_W = A_HEADS * HEAD_DIM
A_KV_W = A_KV_HEADS * HEAD_DIM
B_WIDTH = D_MODEL // 2
C_HEADS = 16
Q_LORA = D_MODEL // 4
KV_LORA = D_MODEL // 8
QK_NOPE = 128
QK_ROPE = 64
V_DIM = 128
C_OUT_W = C_HEADS * V_DIM
D_WIDTH = D_MODEL // 2
D_HEAD_SIZE = 64
D_HEADS = D_WIDTH // D_HEAD_SIZE
DECAY_LORA = 64
AAA_LORA = 64
GATE_LORA = 256
GN_EPS = 64e-5
SHIFT_W = 3 * D_WIDTH + DECAY_LORA + AAA_LORA
N_GROUPS = 8
EXPERTS_PER_GROUP = 8
N_EXPERTS = N_GROUPS * EXPERTS_PER_GROUP
TOP_K = 2
EXPERT_FF = 512

N_CTX_TOK = BATCH * SEQ
N_LAT_TOK = DEC_BATCH * DEC_SEQ
N_TOK = N_CTX_TOK + N_LAT_TOK
N_MOD_GROUPS = 1 + DEC_BATCH

LANES = 128
VMEM_LIMIT_BYTES = 56 * 1024 * 1024
MOE_ROWS = 256
MOE_DMA_GROUP = 8
MOE_TOK_BITS = 13
MOE_TOK_MASK = (1 << MOE_TOK_BITS) - 1
RWKV_CHUNK = 32
RWKV_PAIR = 2 * D_HEAD_SIZE
N_PAIRS = D_WIDTH // RWKV_PAIR


_NN = (((1,), (0,)), ((), ()))
_NT = (((1,), (1,)), ((), ()))
_TN = (((0,), (0,)), ((), ()))


def _cparams(*sem):
    return pltpu.CompilerParams(dimension_semantics=sem, vmem_limit_bytes=VMEM_LIMIT_BYTES)


def _row_group(i, tm):
    nc = N_CTX_TOK // tm
    nl = DEC_SEQ // tm
    return jnp.where(i < nc, 0, 1 + (i - nc) // nl)


def _normmod_kernel(*refs, has_res, emit_x, has_router):
    if has_res:
        x_ref, ya_ref, yb_ref, gate_ref, g_ref, shift_ref, scale_ref, *out_refs = refs
        x = x_ref[...] + gate_ref[0] * (ya_ref[...] + yb_ref[...])
    else:
        x_ref, g_ref, shift_ref, scale_ref, *out_refs = refs
        x = x_ref[...]
    if has_router:
        wr_ref, *out_refs = out_refs
    if emit_x:
        out_refs[0][...] = x
        out_refs = out_refs[1:]
    y = x * lax.rsqrt(jnp.mean(x * x, axis=-1, keepdims=True) + NORM_EPS)
    y = y * g_ref[...]
    y = y * (1 + scale_ref[0]) + shift_ref[0]
    if has_router:
        out_refs[-1][...] = jnp.dot(y, wr_ref[...], preferred_element_type=F32, precision=lax.Precision.HIGHEST)
        out_refs = out_refs[:-1]
    for o in out_refs:
        o[...] = y.astype(o.dtype)


def _normmod(x, g, shift, scale, out_dtypes, res=None, emit_x=False, router=None, tm=128):
    n, d = x.shape
    grp = lambda i: (_row_group(i, tm), 0, 0)
    row = pl.BlockSpec((tm, d), lambda i: (i, 0))
    vec = pl.BlockSpec((1, 1, d), grp)
    in_specs, args = [row], [x]
    if res is not None:
        y2, gate = res
        in_specs += [row, pl.BlockSpec((tm, d), lambda i: (n // tm + i, 0)), vec]
        args += [y2, y2, gate.reshape(-1, 1, d)]
    in_specs += [pl.BlockSpec((1, d), lambda i: (0, 0)), vec, vec]
    args += [g.reshape(1, d), shift.reshape(-1, 1, d), scale.reshape(-1, 1, d)]
    out_dtypes = ((F32,) if emit_x else ()) + tuple(out_dtypes)
    out_specs = [row for _ in out_dtypes]
    out_shape = [jax.ShapeDtypeStruct((n, d), dt) for dt in out_dtypes]
    if router is not None:
        in_specs.append(pl.BlockSpec(router.shape, lambda i: (0, 0)))
        args.append(router)
        out_specs.append(pl.BlockSpec((tm, router.shape[1]), lambda i: (i, 0)))
        out_shape.append(jax.ShapeDtypeStruct((n, router.shape[1]), F32))
    return pl.pallas_call(
        functools.partial(_normmod_kernel, has_res=res is not None, emit_x=emit_x, has_router=router is not None),
        grid=(n // tm,),
        in_specs=in_specs,
        out_specs=out_specs,
        out_shape=out_shape,
        compiler_params=_cparams("parallel"),
        name="normmod",
    )(*args)


def _mm_kernel(*refs, nk, has_res, precision, w_t):
    if has_res:
        a_ref, w_ref, res_ref, gate_ref, o_ref, acc_ref = refs
    else:
        a_ref, w_ref, o_ref, acc_ref = refs
    k = pl.program_id(2)
    a = a_ref[...]
    w = w_ref[...]
    if precision is None:
        a = a.astype(BF16)
        w = w.astype(BF16)
    part = lax.dot_general(a, w, _NT if w_t else _NN, preferred_element_type=F32, precision=precision)

    def finish(acc):
        if has_res:
            acc = res_ref[...] + gate_ref[0] * acc
        o_ref[...] = acc.astype(o_ref.dtype)

    if nk == 1:
        finish(part)
    else:
        @pl.when(k == 0)
        def _():
            acc_ref[...] = part

        @pl.when(k > 0)
        def _():
            acc_ref[...] += part

        @pl.when(k == nk - 1)
        def _():
            finish(acc_ref[...])


def _mm(a, w, w_idx=(), *, out_dtype=F32, res=None, gate=None, precision=None, w_t=False,
        tm=1024, tn=512, tk=4096, name="mm"):
    m, kdim = a.shape
    n = w.shape[-2] if w_t else w.shape[-1]
    assert (w.shape[-1] if w_t else w.shape[-2]) == kdim
    tm, tn, tk = min(tm, m), min(tn, n), min(tk, kdim)
    assert m % tm == 0 and kdim % tk == 0
    nk = kdim // tk
    lead = tuple(w_idx)
    if w_t:
        w_spec = pl.BlockSpec((None,) * len(lead) + (tn, tk), lambda i, j, k: lead + (j, k))
    else:
        w_spec = pl.BlockSpec((None,) * len(lead) + (tk, tn), lambda i, j, k: lead + (k, j))
    in_specs = [pl.BlockSpec((tm, tk), lambda i, j, k: (i, k)), w_spec]
    args = [a, w]
    has_res = res is not None
    if has_res:
        in_specs += [pl.BlockSpec((tm, tn), lambda i, j, k: (i, j)),
                     pl.BlockSpec((1, 1, tn), lambda i, j, k: (_row_group(i, tm), 0, j))]
        args += [res, gate.reshape(N_MOD_GROUPS, 1, n)]
    return pl.pallas_call(
        functools.partial(_mm_kernel, nk=nk, has_res=has_res, precision=precision, w_t=w_t),
        grid=(m // tm, pl.cdiv(n, tn), nk),
        in_specs=in_specs,
        out_specs=pl.BlockSpec((tm, tn), lambda i, j, k: (i, j)),
        out_shape=jax.ShapeDtypeStruct((m, n), out_dtype),
        scratch_shapes=[pltpu.VMEM((tm, tn), F32)],
        compiler_params=_cparams("parallel", "parallel", "arbitrary"),
        name=name,
    )(*args)


def _dot_nt(a, b, precision=None):
    return lax.dot_general(a, b, (((1,), (1,)), ((), ())), preferred_element_type=F32,
                           precision=precision)


def _softmax_read(scores, values, sink_col):
    m = scores[0].max(axis=-1, keepdims=True)
    for s in scores[1:]:
        m = jnp.maximum(m, s.max(axis=-1, keepdims=True))
    if sink_col is not None:
        m = jnp.maximum(m, sink_col)
    den = jnp.exp(sink_col - m) if sink_col is not None else 0.0
    acc = None
    for s, v in zip(scores, values):
        p = jnp.exp(s - m)
        den = den + p.sum(axis=-1, keepdims=True)
        pv = jnp.dot(p.astype(BF16), v, preferred_element_type=F32)
        acc = pv if acc is None else acc + pv
    return acc / den


def _gqa_kernel(*refs, tq, has2, band):
    if has2:
        sink_ref, q1_ref, k1_ref, v1_ref, q2_ref, k2_ref, v2_ref, o_ref = refs
    else:
        sink_ref, q1_ref, k1_ref, v1_ref, o_ref = refs
    h = pl.program_id(1)
    qi = pl.program_id(2)
    scale = HEAD_DIM ** -0.5

    def stack(q_ref):
        q = q_ref[...]
        return jnp.concatenate([q[:, g * HEAD_DIM:(g + 1) * HEAD_DIM] for g in range(A_GROUP)],
                               axis=0).astype(BF16)

    s1 = _dot_nt(stack(q1_ref), k1_ref[...].astype(BF16)) * scale
    if band:
        qpos = qi * tq + lax.broadcasted_iota(jnp.int32, s1.shape, 0) % tq
        kpos = lax.broadcasted_iota(jnp.int32, s1.shape, 1)
        s1 = jnp.where(jnp.abs(kpos - qpos) <= WINDOW, s1, NEG_INF)
    scores, values = [s1], [v1_ref[...].astype(BF16)]
    if has2:
        scores.append(_dot_nt(stack(q2_ref), k2_ref[...].astype(BF16)) * scale)
        values.append(v2_ref[...].astype(BF16))
    sink_col = jnp.concatenate([jnp.full((tq, 1), sink_ref[h * A_GROUP + g], F32)
                                for g in range(A_GROUP)], axis=0)
    out = _softmax_read(scores, values, sink_col)
    for g in range(A_GROUP):
        o_ref[:, g * HEAD_DIM:(g + 1) * HEAD_DIM] = out[g * tq:(g + 1) * tq].astype(o_ref.dtype)


def _tok_spec(rows, width, row0, rows_per_batch, col0, per_q):
    assert row0 % rows == 0 and rows_per_batch % rows == 0 and col0 % width == 0
    r0, rb, c0 = row0 // rows, rows_per_batch // rows, col0 // width
    if per_q:
        return pl.BlockSpec((rows, width), lambda b, h, qi: (r0 + b * rb + qi, c0 + h))
    return pl.BlockSpec((rows, width), lambda b, h, qi: (r0 + b * rb, c0 + h))


def _gqa_attn(sink, q1, k1, v1, seg2=None, *, nb, tq_len, band, out_rows, tq=128):
    gw = A_GROUP * HEAD_DIM
    qspec = lambda t: _tok_spec(tq, gw, t[1], tq_len, t[2], True)
    kspec = lambda t, tk: _tok_spec(tk, HEAD_DIM, t[1], tk, t[2], False)
    in_specs = [pl.BlockSpec(memory_space=pltpu.SMEM), qspec(q1), kspec(k1, tq_len), kspec(v1, tq_len)]
    args = [sink, q1[0], k1[0], v1[0]]
    if seg2 is not None:
        q2, k2, v2, tk2 = seg2
        in_specs += [qspec(q2), kspec(k2, tk2), kspec(v2, tk2)]
        args += [q2[0], k2[0], v2[0]]
    return pl.pallas_call(
        functools.partial(_gqa_kernel, tq=tq, has2=seg2 is not None, band=band),
        grid=(nb, A_KV_HEADS, tq_len // tq),
        in_specs=in_specs,
        out_specs=_tok_spec(tq, gw, 0, tq_len, 0, True),
        out_shape=jax.ShapeDtypeStruct((out_rows, A_Q_W), BF16),
        compiler_params=_cparams("parallel", "parallel", "arbitrary"),
        name="gqa_attn",
    )(*args)


def _mla_kernel(*refs, has2):
    if has2:
        qn_ref, qp1_ref, kn1_ref, kp1_ref, v1_ref, qp2_ref, kn2_ref, kp2_ref, v2_ref, o_ref = refs
    else:
        qn_ref, qp1_ref, kn1_ref, kp1_ref, v1_ref, o_ref = refs
    scale = (QK_NOPE + QK_ROPE) ** -0.5
    lane = lax.broadcasted_iota(jnp.int32, qp1_ref.shape, 1)
    for hh in range(2):
        cols = slice(hh * QK_NOPE, (hh + 1) * QK_NOPE)
        mine = (lane < QK_ROPE) if hh == 0 else (lane >= QK_ROPE)
        qn = qn_ref[:, cols]

        def score(qp_ref, kn_ref, kp_ref):
            q = jnp.concatenate([qn, jnp.where(mine, qp_ref[...], 0.0)], axis=1).astype(BF16)
            k = jnp.concatenate([kn_ref[:, cols], kp_ref[...]], axis=1).astype(BF16)
            return _dot_nt(q, k) * scale

        scores = [score(qp1_ref, kn1_ref, kp1_ref)]
        values = [v1_ref[:, cols].astype(BF16)]
        if has2:
            scores.append(score(qp2_ref, kn2_ref, kp2_ref))
            values.append(v2_ref[:, cols].astype(BF16))
        o_ref[:, cols] = _softmax_read(scores, values, None).astype(o_ref.dtype)


def _mla_attn(qn, qp1, kn1, kp1, v1, seg2=None, *, nb, tq_len, out_rows, tq=256):
    pw = 2 * QK_NOPE
    qspec = lambda t, w: _tok_spec(tq, w, t[1], tq_len, t[2], True)
    kspec = lambda t, w, tk: _tok_spec(tk, w, t[1], tk, t[2], False)
    kpspec = lambda t, tk: pl.BlockSpec((tk, LANES), lambda b, h, qi: (t[1] // tk + b, 0))
    in_specs = [qspec(qn, pw), qspec(qp1, LANES), kspec(kn1, pw, tq_len), kpspec(kp1, tq_len),
                kspec(v1, pw, tq_len)]
    args = [qn[0], qp1[0], kn1[0], kp1[0], v1[0]]
    if seg2 is not None:
        qp2, kn2, kp2, v2, tk2 = seg2
        in_specs += [qspec(qp2, LANES), kspec(kn2, pw, tk2), kpspec(kp2, tk2), kspec(v2, pw, tk2)]
        args += [qp2[0], kn2[0], kp2[0], v2[0]]
    return pl.pallas_call(
        functools.partial(_mla_kernel, has2=seg2 is not None),
        grid=(nb, C_HEADS // 2, tq_len // tq),
        in_specs=in_specs,
        out_specs=_tok_spec(tq, pw, 0, tq_len, 0, True),
        out_shape=jax.ShapeDtypeStruct((out_rows, C_OUT_W), BF16),
        compiler_params=_cparams("parallel", "parallel", "arbitrary"),
        name="mla_attn",
    )(*args)


def _bdot(a, b, dims=_NN):
    return lax.dot_general(a.astype(BF16), b.astype(BF16), dims, preferred_element_type=F32)


def _bf16_parts(x, n):
    parts = []
    for _ in range(n - 1):
        hi = x.astype(BF16)
        parts.append(hi)
        x = x - hi.astype(F32)
    parts.append(x.astype(BF16))
    return parts


def _rwkv_kernel(r_ref, lw_ref, k_ref, v_ref, a_ref, b_ref, s0_ref, y_ref, s_ref,
                 p_scr, q_scr, rw_scr, y0_scr, *, seq, pairs, reverse, unroll):
    c = RWKV_CHUNK
    nchunk = seq // c
    gl = 2 * RWKV_PAIR
    hg = gl // D_HEAD_SIZE
    cs = hg * c
    width = pairs * RWKV_PAIR
    row = lax.broadcasted_iota(jnp.int32, (c, c), 0)
    col = lax.broadcasted_iota(jnp.int32, (c, c), 1)
    tri = ((col >= row) if reverse else (col <= row)).astype(BF16)
    last = 0 if reverse else c - 1
    head_of_lane = lax.broadcasted_iota(jnp.int32, (c, gl), 1) // D_HEAD_SIZE
    grow = lax.broadcasted_iota(jnp.int32, (cs, 2 * cs), 0)
    gcol = lax.broadcasted_iota(jnp.int32, (cs, 2 * cs), 1)
    gcol_in = jnp.where(gcol >= cs, gcol - cs, gcol)
    before = (gcol_in > grow) if reverse else (gcol_in < grow)
    upto = (gcol_in >= grow) if reverse else (gcol_in <= grow)
    eye = (lax.broadcasted_iota(jnp.int32, (cs, cs), 0) ==
           lax.broadcasted_iota(jnp.int32, (cs, cs), 1)).astype(F32)
    eye_pair = (lax.broadcasted_iota(jnp.int32, (RWKV_PAIR, RWKV_PAIR), 0) ==
                lax.broadcasted_iota(jnp.int32, (RWKV_PAIR, RWKV_PAIR), 1))
    n_double = c.bit_length() - 2

    def stacked(x):
        return jnp.concatenate([jnp.where(head_of_lane == h, x, 0.0) for h in range(hg)], axis=0)

    def local(step, carry):
        items = []
        for ui in range(unroll):
            cidx = step * unroll + ui
            rows = pl.ds(pl.multiple_of(cidx * c, c), c)
            r, lw, k = r_ref[0, rows, :], lw_ref[0, rows, :], k_ref[0, rows, :]
            v, a, b = v_ref[0, rows, :], a_ref[0, rows, :], b_ref[0, rows, :]
            cum = jnp.dot(tri, jnp.concatenate(_bf16_parts(lw, 3), axis=1), preferred_element_type=F32)
            lg = cum[:, :width] + (cum[:, width:2 * width] + cum[:, 2 * width:])
            lg_end = lg[last:last + 1]
            g_inv = jnp.exp(-lg)
            g_end = jnp.exp(lg_end - lg)
            g_chunk = jnp.exp(lg_end)
            full = (a * jnp.exp(lg - lw), r * jnp.exp(lg), b * g_inv, k * g_inv, v, b * g_end, k * g_end)
            for g in range(pairs // 2):
                ops = tuple(stacked(x[:, g * gl:(g + 1) * gl]) for x in full)
                items.append((cidx, g, g_chunk) + ops)
        n = range(len(items))
        at, rt, bt, kt, vs, bh, kh = ([it[3 + j] for it in items] for j in range(7))
        gram = [_bdot(jnp.concatenate([at[i], rt[i]], axis=0), jnp.concatenate([bt[i], kt[i]], axis=0), _NT)
                for i in n]
        g_u = [jnp.where(before, gram[i][:cs], 0.0) for i in n]
        g_y = [jnp.where(upto, gram[i][cs:], 0.0) for i in n]
        low = [g_u[i][:, :cs] for i in n]
        tinv = [eye + low[i] for i in n]
        pw = [_bdot(low[i], low[i]) for i in n]
        c0 = [_bdot(g_u[i][:, cs:], vs[i]) for i in n]
        for it in range(n_double):
            if it < n_double - 1:
                both = [_bdot(jnp.concatenate([pw[i], tinv[i]], axis=0), pw[i]) for i in n]
                pw = [both[i][:cs] for i in n]
                tinv = [tinv[i] + both[i][cs:] for i in n]
            else:
                tinv = [tinv[i] + _bdot(tinv[i], pw[i]) for i in n]
        wu = [_bdot(tinv[i], jnp.concatenate([at[i], c0[i]], axis=1)) for i in n]
        pmat = [_bdot(wu[i][:, :gl], bh[i], _TN) for i in n]
        qmat = [_bdot(jnp.concatenate([wu[i][:, gl:], vs[i]], axis=0),
                      jnp.concatenate([bh[i], kh[i]], axis=0), _TN) for i in n]
        ry = [_bdot(g_y[i], jnp.concatenate([wu[i], jnp.concatenate([jnp.zeros_like(vs[i]), vs[i]], axis=1)],
                                            axis=0)) for i in n]
        for i in n:
            cidx, g, g_chunk = items[i][:3]
            rw, y0 = rt[i] + ry[i][:, :gl], ry[i][:, gl:]
            for j in range(2):
                p = 2 * g + j
                lanes = slice(j * RWKV_PAIR, (j + 1) * RWKV_PAIR)
                head_rows = slice(2 * j * c, (2 * j + 2) * c)
                decay = g_chunk[:, p * RWKV_PAIR:(p + 1) * RWKV_PAIR]
                p_scr[cidx, p] = jnp.where(eye_pair, decay, 0.0) + pmat[i][lanes, lanes]
                q_scr[cidx, p] = qmat[i][lanes, lanes]
                rw_scr[cidx, p] = rw[head_rows, lanes]
                y0_scr[cidx, p] = y0[head_rows, lanes]
        return carry

    lax.fori_loop(0, nchunk // unroll, local, 0)
    s_ref[0] = s0_ref[0]

    def scan(ci, carry):
        cidx = (nchunk - 1 - ci) if reverse else ci
        rows = pl.ds(pl.multiple_of(cidx * c, c), c)
        s = [s_ref[0, p] for p in range(pairs)]
        ys = [_bdot(rw_scr[cidx, p], s[p], _NT) for p in range(pairs)]
        s_new = [_bdot(s[p], p_scr[cidx, p]) for p in range(pairs)]
        for p in range(pairs):
            y = ys[p] + y0_scr[cidx, p]
            y_ref[0, rows, p * RWKV_PAIR:(p + 1) * RWKV_PAIR] = y[:c] + y[c:]
            s_ref[0, p] = s_new[p] + q_scr[cidx, p]
        return carry

    lax.fori_loop(0, nchunk, scan, 0)


def _rwkv_scan(r, lw, k, v, a, b, s0, *, reverse, pairs, unroll):
    nb, seq, _ = r.shape
    assert 4 * RWKV_CHUNK == RWKV_PAIR and seq % (RWKV_CHUNK * unroll) == 0
    assert pairs % 2 == 0 and N_PAIRS % pairs == 0
    nchunk = seq // RWKV_CHUNK
    tok = pl.BlockSpec((1, seq, pairs * RWKV_PAIR), lambda bi, pi: (bi, 0, pi))
    st = pl.BlockSpec((1, pairs, RWKV_PAIR, RWKV_PAIR), lambda bi, pi: (bi, pi, 0, 0))
    pair_mat = pltpu.VMEM((nchunk, pairs, RWKV_PAIR, RWKV_PAIR), F32)
    pair_rows = pltpu.VMEM((nchunk, pairs, 2 * RWKV_CHUNK, RWKV_PAIR), F32)
    return pl.pallas_call(
        functools.partial(_rwkv_kernel, seq=seq, pairs=pairs, reverse=reverse, unroll=unroll),
        grid=(nb, N_PAIRS // pairs),
        in_specs=[tok] * 6 + [st],
        out_specs=[tok, st],
        out_shape=[jax.ShapeDtypeStruct(r.shape, F32), jax.ShapeDtypeStruct(s0.shape, F32)],
        scratch_shapes=[pair_mat, pair_mat, pair_rows, pair_rows],
        compiler_params=_cparams("parallel", "parallel"),
        name="rwkv_rev" if reverse else "rwkv_fwd",
    )(r, lw, k, v, a, b, s0)


def _moe_kernel(be_ref, nv_ref, cnt_ref, idx_ref, w_ref, h_hbm, wg_ref, wu_ref, wd_ref, y_hbm,
                xbuf, gacc, uacc, hbuf, acc, gsem, ssem):
    i = pl.program_id(0)
    f = pl.program_id(1)
    n_used = nv_ref[0]
    cur = i % 2

    def row_in(blk, r, dst):
        tok = idx_ref[blk * MOE_ROWS + r] & MOE_TOK_MASK
        return pltpu.make_async_copy(h_hbm.at[pl.ds(tok, 1)], xbuf.at[dst, pl.ds(r, 1)], gsem.at[dst])

    def row_out(r):
        slot = idx_ref[i * MOE_ROWS + r] >> MOE_TOK_BITS
        return pltpu.make_async_copy(acc.at[pl.ds(r, 1)], y_hbm.at[pl.ds(slot, 1)], ssem)

    def groups(blk):
        return (cnt_ref[blk] + MOE_DMA_GROUP - 1) // MOE_DMA_GROUP

    def gather(blk, dst):
        def body(g, carry):
            for r in range(MOE_DMA_GROUP):
                row_in(blk, g * MOE_DMA_GROUP + r, dst).start()
            return carry
        lax.fori_loop(0, groups(blk), body, 0)

    @pl.when((f == 0) & (i == 0))
    def _():
        xbuf[...] = jnp.zeros_like(xbuf)

    @pl.when((f == 0) & (i == 0) & (n_used > 0))
    def _():
        gather(0, 0)

    @pl.when((f == 0) & (i < n_used))
    def _():
        n = pl.multiple_of(groups(i) * MOE_DMA_GROUP, MOE_DMA_GROUP)
        pltpu.make_async_copy(h_hbm.at[pl.ds(0, n)], xbuf.at[cur, pl.ds(0, n)], gsem.at[cur]).wait()

        @pl.when(i + 1 < n_used)
        def _():
            gather(i + 1, 1 - cur)

    half_d = xbuf.shape[2] // 2
    half_ff = hbuf.shape[1] // 2

    def gate_up(lo):
        x = xbuf[cur, :, lo:lo + half_d].astype(BF16)
        return (jnp.dot(x, wg_ref[...].astype(BF16), preferred_element_type=F32),
                jnp.dot(x, wu_ref[...].astype(BF16), preferred_element_type=F32))

    def down(lo):
        return jnp.dot(hbuf[:, lo:lo + half_ff], wd_ref[...].astype(BF16), preferred_element_type=F32)

    for parity, (d_lo, d_hi, ff_lo, ff_hi) in enumerate(((0, half_d, 0, half_ff), (half_d, 0, half_ff, 0))):
        mine = (i < n_used) & (i % 2 == parity)

        @pl.when(mine & (f == 0))
        def _():
            gacc[...], uacc[...] = gate_up(d_lo)

        @pl.when(mine & (f == 1))
        def _():
            g, u = gate_up(d_hi)
            g, u = gacc[...] + g, uacc[...] + u
            hbuf[...] = ((g * jax.nn.sigmoid(g)) * u).astype(BF16)
            acc[...] = down(ff_lo)

        @pl.when(mine & (f == 2))
        def _():
            acc[...] = (acc[...] + down(ff_hi)) * w_ref[...]

    @pl.when((i < n_used) & (f == 2))
    def _():
        n = cnt_ref[i]
        whole = n // MOE_DMA_GROUP

        def body(g, carry):
            for r in range(MOE_DMA_GROUP):
                row_out(g * MOE_DMA_GROUP + r).start()
            return carry
        lax.fori_loop(0, whole, body, 0)

        def rest(r, carry):
            row_out(r).start()
            return carry
        lax.fori_loop(whole * MOE_DMA_GROUP, n, rest, 0)

        @pl.when(whole > 0)
        def _():
            m = pl.multiple_of(whole * MOE_DMA_GROUP, MOE_DMA_GROUP)
            pltpu.make_async_copy(acc.at[pl.ds(0, m)], y_hbm.at[pl.ds(0, m)], ssem).wait()

        def rest_wait(r, carry):
            row_out(r).wait()
            return carry
        lax.fori_loop(whole * MOE_DMA_GROUP, n, rest_wait, 0)


def _moe_experts(layer, block_e, n_used, row_cnt, row_idx, row_w, h, w_gate, w_up, w_down):
    n_blocks = block_e.shape[0]
    n_tok, d = h.shape
    ff = w_gate.shape[-1]
    flip = lambda i, half: jnp.where(i % 2 == 0, half, 1 - half)
    in_half = pl.BlockSpec((None, None, d // 2, ff),
                           lambda i, f, be, nv, ct, ix: (layer, be[i], flip(i, jnp.minimum(f, 1)), 0))
    return pl.pallas_call(
        _moe_kernel,
        grid_spec=pltpu.PrefetchScalarGridSpec(
            num_scalar_prefetch=4,
            grid=(n_blocks, 3),
            in_specs=[pl.BlockSpec((MOE_ROWS, 1), lambda i, f, be, nv, ct, ix: (i, 0)),
                      pl.BlockSpec(memory_space=pl.ANY),
                      in_half, in_half,
                      pl.BlockSpec((None, None, ff // 2, d),
                                   lambda i, f, be, nv, ct, ix: (layer, be[i], flip(i, jnp.maximum(f - 1, 0)), 0))],
            out_specs=pl.BlockSpec(memory_space=pl.ANY),
            scratch_shapes=[pltpu.VMEM((2, MOE_ROWS, d), F32), pltpu.VMEM((MOE_ROWS, ff), F32),
                            pltpu.VMEM((MOE_ROWS, ff), F32), pltpu.VMEM((MOE_ROWS, ff), BF16),
                            pltpu.VMEM((MOE_ROWS, d), F32),
                            pltpu.SemaphoreType.DMA((2,)), pltpu.SemaphoreType.DMA(())]),
        out_shape=jax.ShapeDtypeStruct((n_tok * TOP_K, d), F32),
        compiler_params=_cparams("arbitrary", "arbitrary"),
        name="moe_experts",
    )(block_e, n_used, row_cnt, row_idx, row_w, h, w_gate, w_up, w_down)


def _moe(layer, h_f32, logits, b_router, w_gate, w_up, w_down):
    n_tok = h_f32.shape[0]
    logits = logits[:, :N_GROUPS + N_EXPERTS] + b_router
    gp = jax.nn.softmax(logits[:, :N_GROUPS], axis=-1)
    g_idx = jnp.argmax(gp, axis=-1)[:, None]
    p_g = jnp.max(gp, axis=-1, keepdims=True)
    el = logits[:, N_GROUPS:].reshape(-1, N_GROUPS, EXPERTS_PER_GROUP)
    el = jnp.take_along_axis(el, g_idx[:, :, None], axis=1)[:, 0]
    pe = jax.nn.softmax(el, axis=-1)
    e_first = jnp.argmax(pe, axis=-1)[:, None]
    pe_rest = jnp.where(jnp.arange(EXPERTS_PER_GROUP)[None, :] == e_first, -jnp.inf, pe)
    e_loc = jnp.concatenate([e_first, jnp.argmax(pe_rest, axis=-1)[:, None]], axis=1)
    p_e = jnp.concatenate([jnp.max(pe, axis=-1, keepdims=True), jnp.max(pe_rest, axis=-1, keepdims=True)], axis=1)
    wts = p_g * p_e / jnp.sum(p_e, axis=-1, keepdims=True)
    idx = g_idx * EXPERTS_PER_GROUP + e_loc

    n_slots = n_tok * TOP_K
    assert n_tok <= MOE_TOK_MASK and n_slots % MOE_ROWS == 0
    flat_e = idx.reshape(-1).astype(jnp.int32)
    order = jnp.argsort(flat_e).astype(jnp.int32)
    sorted_e = flat_e[order]
    counts = jnp.bincount(flat_e, length=N_EXPERTS)
    padded = (counts + MOE_ROWS - 1) // MOE_ROWS * MOE_ROWS
    pad_end = jnp.cumsum(padded)
    pad_start = pad_end - padded
    start = jnp.cumsum(counts) - counts
    dest_sorted = (pad_start[sorted_e] + jnp.arange(n_slots) - start[sorted_e]).astype(jnp.int32)
    n_blocks = n_slots // MOE_ROWS + N_EXPERTS
    n_rows = n_blocks * MOE_ROWS
    tok = order // TOP_K
    slot = (order % TOP_K) * n_tok + tok
    row_idx = jnp.zeros((n_rows,), jnp.int32).at[dest_sorted].set(tok | (slot << MOE_TOK_BITS))
    row_w = jnp.zeros((n_rows,), F32).at[dest_sorted].set(wts.reshape(-1)[order])
    blk_start = jnp.arange(n_blocks) * MOE_ROWS
    block_e = jnp.minimum(jnp.searchsorted(pad_end, blk_start, side='right'), N_EXPERTS - 1).astype(jnp.int32)
    row_cnt = jnp.clip(counts[block_e] - (blk_start - pad_start[block_e]), 0, MOE_ROWS).astype(jnp.int32)
    n_used = (pad_end[-1] // MOE_ROWS).astype(jnp.int32).reshape(1)
    return _moe_experts(layer, block_e, n_used, row_cnt, row_idx, row_w.reshape(n_rows, 1), h_f32,
                        w_gate, w_up, w_down)


def _axial_rope_tables(n_tok, rot_dim):
    t = jnp.arange(n_tok)
    rows = (t // GRID_W).astype(F32)
    cols = (t % GRID_W).astype(F32)
    axis_dim = rot_dim // 2
    inv = ROPE_BASE ** (-jnp.arange(0, axis_dim, 2, dtype=F32) / axis_dim)
    ang_r = rows[:, None] * inv[None, :]
    ang_c = cols[:, None] * inv[None, :]
    return jnp.cos(ang_r), jnp.sin(ang_r), jnp.cos(ang_c), jnp.sin(ang_c)


def _rope_rotate(x, cos, sin):
    half = x.shape[-1] // 2
    c, s = cos[None, :, None, :], sin[None, :, None, :]
    x1, x2 = x[..., :half], x[..., half:]
    return jnp.concatenate([x1 * c - x2 * s, x2 * c + x1 * s], axis=-1)


def _apply_axial_rope(x, tables):
    cos_r, sin_r, cos_c, sin_c = tables
    h = x.shape[-1] // 2
    return jnp.concatenate([_rope_rotate(x[..., :h], cos_r, sin_r),
                            _rope_rotate(x[..., h:], cos_c, sin_c)], axis=-1)


def _conv_kernel(b_ref, c_ref, u_ref, cp_ref, up_ref, cn_ref, un_ref, w_ref, o_ref, *, tm):
    i = pl.program_id(0)
    nc = N_CTX_TOK // tm
    in_seq = jnp.where(i < nc, i % (SEQ // tm), (i - nc) % (DEC_SEQ // tm))
    last_in_seq = jnp.where(i < nc, SEQ // tm - 1, DEC_SEQ // tm - 1)
    z = c_ref[...] * u_ref[...]
    z_before = jnp.where(in_seq == 0, 0.0, cp_ref[7:8, :] * up_ref[7:8, :])
    z_after = jnp.where(in_seq == last_in_seq, 0.0, cn_ref[0:1, :] * un_ref[0:1, :])
    row = lax.broadcasted_iota(jnp.int32, (tm, 1), 0)
    z_prev = jnp.where(row == 0, z_before, pltpu.roll(z, 1, axis=0))
    z_next = jnp.where(row == tm - 1, z_after, pltpu.roll(z, tm - 1, axis=0))
    conv = w_ref[0:1, :] * z_prev + w_ref[1:2, :] * z + w_ref[2:3, :] * z_next
    o_ref[...] = (b_ref[...] * conv).astype(o_ref.dtype)


def _short_conv(p, col0, conv_w, tm=128, tw=1024):
    assert col0 % tw == 0 and B_WIDTH % tw == 0 and SEQ % tm == 0 and tm % 8 == 0
    c0, nw, g8, last8 = col0 // tw, B_WIDTH // tw, tm // 8, p.shape[0] // 8 - 1
    tile = lambda n: pl.BlockSpec((tm, tw), lambda i, c: (i, c0 + n * nw + c))
    before = lambda n: pl.BlockSpec((8, tw), lambda i, c: (jnp.maximum(i * g8 - 1, 0), c0 + n * nw + c))
    after = lambda n: pl.BlockSpec((8, tw), lambda i, c: (jnp.minimum((i + 1) * g8, last8), c0 + n * nw + c))
    return pl.pallas_call(
        functools.partial(_conv_kernel, tm=tm),
        grid=(p.shape[0] // tm, nw),
        in_specs=[tile(0), tile(1), tile(2), before(1), before(2), after(1), after(2),
                  pl.BlockSpec((3, tw), lambda i, c: (0, c))],
        out_specs=pl.BlockSpec((tm, tw), lambda i, c: (i, c)),
        out_shape=jax.ShapeDtypeStruct((p.shape[0], B_WIDTH), BF16),
        compiler_params=_cparams("parallel", "parallel"),
        name="short_conv",
    )(p, p, p, p, p, p, p, conv_w)


def _out_proj_kernel(a1_ref, a2_ref, w_ref, res_ref, gate_ref, o_ref):
    k1 = a1_ref.shape[1]
    w = w_ref[...].astype(BF16)
    acc = (jnp.dot(a1_ref[...], w[:k1], preferred_element_type=F32) +
           jnp.dot(a2_ref[...], w[k1:], preferred_element_type=F32))
    o_ref[...] = res_ref[...] + gate_ref[0] * acc


def _out_proj(a1, a2, w, w_idx, res, gate, tm=1024, tn=512, name="out_proj"):
    m, k1 = a1.shape
    k2 = a2.shape[1]
    n = w.shape[-1]
    assert w.shape[-2] == k1 + k2 and m % tm == 0 and n % tn == 0
    lead = tuple(w_idx)
    return pl.pallas_call(
        _out_proj_kernel,
        grid=(m // tm, n // tn),
        in_specs=[pl.BlockSpec((tm, k1), lambda i, j: (i, 0)), pl.BlockSpec((tm, k2), lambda i, j: (i, 0)),
                  pl.BlockSpec((None,) * len(lead) + (k1 + k2, tn), lambda i, j: lead + (0, j)),
                  pl.BlockSpec((tm, tn), lambda i, j: (i, j)),
                  pl.BlockSpec((1, 1, tn), lambda i, j: (_row_group(i, tm), 0, j))],
        out_specs=pl.BlockSpec((tm, tn), lambda i, j: (i, j)),
        out_shape=jax.ShapeDtypeStruct((m, n), F32),
        compiler_params=_cparams("parallel", "parallel"),
        name=name,
    )(a1, a2, w, res, gate.reshape(N_MOD_GROUPS, 1, n))


def _even_mixer(h, i, cache_k, cache_v, w_in, w_out, sink, conv_w, x, gate):
    p = _mm(h, w_in, (i,), name="even_in")
    o_k, o_v, o_b = A_Q_W, A_Q_W + A_KV_W, A_Q_W + 2 * A_KV_W
    att_c = _gqa_attn(sink[i], (p, 0, 0), (p, 0, o_k), (p, 0, o_v),
                      nb=BATCH, tq_len=SEQ, band=False, out_rows=N_CTX_TOK, tq=SEQ)
    pl_ = p[N_CTX_TOK:]
    tabs = _axial_rope_tables(DEC_SEQ, HEAD_DIM)
    q_rot = _apply_axial_rope(pl_[:, :A_Q_W].reshape(DEC_BATCH, DEC_SEQ, A_HEADS, HEAD_DIM), tabs)
    k_rot = _apply_axial_rope(pl_[:, o_k:o_v].reshape(DEC_BATCH, DEC_SEQ, A_KV_HEADS, HEAD_DIM), tabs)
    ck = cache_k[:, i].reshape(DEC_BATCH * PAST_LEN, A_KV_W)
    cv = cache_v[:, i].reshape(DEC_BATCH * PAST_LEN, A_KV_W)
    att_l = _gqa_attn(sink[i], (q_rot.reshape(N_LAT_TOK, A_Q_W), 0, 0),
                      (k_rot.reshape(N_LAT_TOK, A_KV_W), 0, 0), (p, N_CTX_TOK, o_v),
                      ((p, N_CTX_TOK, 0), (ck, 0, 0), (cv, 0, 0), PAST_LEN),
                      nb=DEC_BATCH, tq_len=DEC_SEQ, band=True, out_rows=N_LAT_TOK)
    conv = _short_conv(p, o_b, conv_w[i])
    x = _out_proj(jnp.concatenate([att_c, att_l], axis=0), conv, w_out, (i,), x, gate, name="even_out")
    new_k = p[:N_CTX_TOK, o_k:o_v].reshape(BATCH, SEQ, A_KV_HEADS, HEAD_DIM)
    new_v = p[:N_CTX_TOK, o_v:o_b].reshape(BATCH, SEQ, A_KV_HEADS, HEAD_DIM)
    return x, new_k, new_v


def _rms(x, g):
    return x * lax.rsqrt(jnp.mean(x * x, axis=-1, keepdims=True) + NORM_EPS) * g


def _head_sums(x):
    rows, width = x.shape
    nt = width // LANES
    ri = lax.broadcasted_iota(jnp.int32, (LANES, LANES), 0) // D_HEAD_SIZE
    ci = lax.broadcasted_iota(jnp.int32, (LANES, LANES), 1) // D_HEAD_SIZE
    same_head = (ri == ci).astype(BF16)
    stacked = jnp.concatenate([part[:, j * LANES:(j + 1) * LANES] for part in _bf16_parts(x, 3) for j in range(nt)],
                              axis=0)
    s = jnp.dot(stacked, same_head, preferred_element_type=F32)
    n = nt * rows
    s = s[:n] + (s[n:2 * n] + s[2 * n:])
    return jnp.concatenate([s[j * rows:(j + 1) * rows] for j in range(nt)], axis=1)


def _rwkv_prep_kernel(x_ref, edge_ref, mu_ref, w0_ref, a0_ref, kk_ref, ka_ref, wup_ref, aup_ref,
                      r_ref, lw_ref, k_ref, v_ref, a_ref, b_ref, *, tm, seq, reverse):
    j = pl.program_id(0)
    blocks_per_seq = seq // tm
    x = x_ref[...]
    row = lax.broadcasted_iota(jnp.int32, (tm, 1), 0)
    if reverse:
        edge = jnp.where(j % blocks_per_seq == blocks_per_seq - 1, 0.0, edge_ref[0:1, :])
        nbr = jnp.where(row == tm - 1, edge, pltpu.roll(x, tm - 1, axis=0))
    else:
        edge = jnp.where(j % blocks_per_seq == 0, 0.0, edge_ref[7:8, :])
        nbr = jnp.where(row == 0, edge, pltpu.roll(x, 1, axis=0))
    z = x + mu_ref[...] * (nbr - x)
    r, k, v = (z[:, n * D_WIDTH:(n + 1) * D_WIDTH] for n in range(3))
    tail = z[:, 3 * D_WIDTH:]
    w_raw = w0_ref[...] + _bdot(jnp.tanh(tail), wup_ref[...])
    sp = jnp.maximum(-w_raw, 0.0) + jnp.log(1.0 + jnp.exp(-jnp.abs(w_raw)))
    lw_ref[...] = -jnp.exp(-sp - 0.5)
    a = jax.nn.sigmoid(a0_ref[...] + _bdot(tail, aup_ref[...]))
    kk = k * kk_ref[...]
    kk = kk / jnp.maximum(jnp.sqrt(_head_sums(kk * kk)), 1e-12)
    r_ref[...] = r
    k_ref[...] = k * (1 + (a - 1) * ka_ref[...])
    v_ref[...] = v
    a_ref[...] = -kk
    b_ref[...] = kk * a


def _rwkv_prep(rw, row0, nb, seq, mu, w0, a0, k_k, k_a, wup_pad, aup_pad, *, reverse, tm=128):
    rows = nb * seq
    assert row0 % tm == 0 and seq % tm == 0 and tm % 8 == 0
    b0, g8 = row0 // tm, tm // 8
    if reverse:
        edge = pl.BlockSpec((8, SHIFT_W), lambda j: (jnp.minimum((b0 + j + 1) * g8, rw.shape[0] // 8 - 1), 0))
    else:
        edge = pl.BlockSpec((8, SHIFT_W), lambda j: (jnp.maximum((b0 + j) * g8 - 1, 0), 0))
    vec = lambda n: pl.BlockSpec((1, n), lambda j: (0, 0))
    lora = pl.BlockSpec((LANES, D_WIDTH), lambda j: (0, 0))
    out = pl.BlockSpec((tm, D_WIDTH), lambda j: (j, 0))
    return pl.pallas_call(
        functools.partial(_rwkv_prep_kernel, tm=tm, seq=seq, reverse=reverse),
        grid=(rows // tm,),
        in_specs=[pl.BlockSpec((tm, SHIFT_W), lambda j: (b0 + j, 0)), edge, vec(SHIFT_W)] + [vec(D_WIDTH)] * 4 +
                 [lora, lora],
        out_specs=[out] * 6,
        out_shape=[jax.ShapeDtypeStruct((rows, D_WIDTH), F32)] * 6,
        compiler_params=_cparams("parallel"),
        name="rwkv_prep",
    )(rw, rw, mu.reshape(1, -1), w0.reshape(1, -1), a0.reshape(1, -1), k_k.reshape(1, -1), k_a.reshape(1, -1),
      wup_pad, aup_pad)


def _rwkv_readout_kernel(*refs):
    dirs = (refs[0:4], refs[4:8])
    glo_ref, gup_ref, rk_ref, lnw_ref, lnb_ref, o_ref = refs[8:]
    total = None
    for d, (y_ref, r_ref, k_ref, v_ref) in enumerate(dirs):
        y = y_ref[...]
        dev = y - _head_sums(y) * (1.0 / D_HEAD_SIZE)
        var = _head_sums(dev * dev) * (1.0 / D_HEAD_SIZE)
        yn = dev * lax.rsqrt(var + GN_EPS) * lnw_ref[...] + lnb_ref[...]
        out = yn + _head_sums(r_ref[...] * k_ref[...] * rk_ref[d:d + 1, :]) * v_ref[...]
        total = out if total is None else total + out
    gate = _bdot(jax.nn.sigmoid(glo_ref[...]), gup_ref[...])
    o_ref[...] = (total * gate).astype(o_ref.dtype)


def _rwkv_readout(per_dir, g_lo, row0, g_up, r_k, ln_w, ln_b, tm=128):
    rows = per_dir[0][0].shape[0]
    tok = pl.BlockSpec((tm, D_WIDTH), lambda j: (j, 0))
    vec = pl.BlockSpec((1, D_WIDTH), lambda j: (0, 0))
    return pl.pallas_call(
        _rwkv_readout_kernel,
        grid=(rows // tm,),
        in_specs=[tok] * 8 + [pl.BlockSpec((tm, GATE_LORA), lambda j: (row0 // tm + j, 0)),
                              pl.BlockSpec((GATE_LORA, D_WIDTH), lambda j: (0, 0)),
                              pl.BlockSpec((2, D_WIDTH), lambda j: (0, 0)), vec, vec],
        out_specs=tok,
        out_shape=jax.ShapeDtypeStruct((rows, D_WIDTH), BF16),
        compiler_params=_cparams("parallel"),
        name="rwkv_readout",
    )(*per_dir[0], *per_dir[1], g_lo, g_up, r_k.reshape(2, D_WIDTH), ln_w.reshape(1, -1), ln_b.reshape(1, -1))


def _rwkv_group(rw, g_lo, row0, s0, i, rp, nb, seq, pairs):
    per_dir, finals = [], []
    zeros = jnp.zeros((LANES - DECAY_LORA, D_WIDTH), F32)
    for d in range(2):
        wup_pad = jnp.concatenate([rp['w_up'][i, d], zeros], axis=0)
        aup_pad = jnp.concatenate([zeros, rp['a_up'][i, d]], axis=0)
        r, lw, k, v, a, b = _rwkv_prep(rw, row0, nb, seq, rp['mu'][i, d], rp['w0'][i, d], rp['a0'][i, d],
                                       rp['k_k'][i, d], rp['k_a'][i, d], wup_pad, aup_pad, reverse=(d == 1))
        to3 = lambda t: t.reshape(nb, seq, D_WIDTH)
        sd = s0[:, d].reshape(nb, N_PAIRS, 2, D_HEAD_SIZE, D_HEAD_SIZE)
        eye2 = jnp.eye(2, dtype=F32)
        s_bd = jnp.einsum('bphij,hg->bphigj', sd, eye2).reshape(nb, N_PAIRS, RWKV_PAIR, RWKV_PAIR)
        y, s_fin = _rwkv_scan(to3(r), to3(lw), to3(k), to3(v), to3(a), to3(b), s_bd,
                              reverse=(d == 1), pairs=pairs, unroll=16 // pairs)
        s_fin = s_fin.reshape(nb, N_PAIRS, 2, D_HEAD_SIZE, 2, D_HEAD_SIZE)
        s_fin = jnp.stack([s_fin[:, :, 0, :, 0], s_fin[:, :, 1, :, 1]], axis=2)
        finals.append(s_fin.reshape(nb, D_HEADS, D_HEAD_SIZE, D_HEAD_SIZE))
        per_dir.append((y.reshape(nb * seq, D_WIDTH), r, k, v))
    out = _rwkv_readout(per_dir, g_lo, row0, rp['g_up'][i], rp['r_k'][i], rp['ln_w'][i], rp['ln_b'][i])
    return out, jnp.stack(finals, axis=1)


def _odd_mixer(h, i, cache_ckv, cache_kpe, state, w_in_t, w_out, mp, rp, x, gate):
    p = _mm(h, w_in_t, (i,), w_t=True, name="odd_in")
    o1 = Q_LORA
    o2 = o1 + KV_LORA
    o3 = o2 + QK_ROPE
    o4 = o3 + SHIFT_W
    q_down, kv_down, k_pe, rw, g_lo = p[:, :o1], p[:, o1:o2], p[:, o2:o3], p[:, o3:o4], p[:, o4:]
    w_uq = mp['w_uq'][i].reshape(Q_LORA, C_HEADS, QK_NOPE + QK_ROPE)
    w_uq = jnp.concatenate([w_uq[:, :, :QK_NOPE].reshape(Q_LORA, -1), w_uq[:, :, QK_NOPE:].reshape(Q_LORA, -1)],
                           axis=1)
    w_ukv = mp['w_ukv'][i].reshape(KV_LORA, C_HEADS, QK_NOPE + V_DIM)
    w_ukv = jnp.concatenate([w_ukv[:, :, :QK_NOPE].reshape(KV_LORA, -1), w_ukv[:, :, QK_NOPE:].reshape(KV_LORA, -1)],
                            axis=1)
    q = _mm(_rms(q_down, mp['q_norm'][i]), w_uq, name="mla_uq")
    c_kv = _rms(kv_down, mp['kv_norm'][i])
    kv = _mm(c_kv, w_ukv, name="mla_ukv")
    kv_ctx = _mm(cache_ckv[:, i].reshape(DEC_BATCH * PAST_LEN, KV_LORA), w_ukv, name="mla_ukv_cache")
    n_nope = C_HEADS * QK_NOPE
    dup = lambda t: jnp.concatenate([t, t], axis=-1)
    kp = dup(k_pe)
    att_c = _mla_attn((q, 0, 0), (q, 0, n_nope), (kv, 0, 0), (kp, 0, 0), (kv, 0, n_nope),
                      nb=BATCH, tq_len=SEQ, out_rows=N_CTX_TOK)
    tabs = _axial_rope_tables(DEC_SEQ, QK_ROPE)
    qpe_rot = _apply_axial_rope(q[N_CTX_TOK:, n_nope:].reshape(DEC_BATCH, DEC_SEQ, C_HEADS, QK_ROPE), tabs)
    kpe_rot = _apply_axial_rope(k_pe[N_CTX_TOK:].reshape(DEC_BATCH, DEC_SEQ, 1, QK_ROPE), tabs)
    kp_ctx = dup(cache_kpe[:, i].reshape(DEC_BATCH * PAST_LEN, QK_ROPE))
    att_l = _mla_attn((q, N_CTX_TOK, 0), (qpe_rot.reshape(N_LAT_TOK, -1), 0, 0), (kv, N_CTX_TOK, 0),
                      (dup(kpe_rot.reshape(N_LAT_TOK, QK_ROPE)), 0, 0), (kv, N_CTX_TOK, n_nope),
                      ((q, N_CTX_TOK, n_nope), (kv_ctx, 0, 0), (kp_ctx, 0, 0), (kv_ctx, 0, n_nope), PAST_LEN),
                      nb=DEC_BATCH, tq_len=DEC_SEQ, out_rows=N_LAT_TOK)
    s_zero = jnp.zeros((BATCH, 2, D_HEADS, D_HEAD_SIZE, D_HEAD_SIZE), F32)
    rw_c, s_c = _rwkv_group(rw, g_lo, 0, s_zero, i, rp, BATCH, SEQ, 8)
    rw_l, _ = _rwkv_group(rw, g_lo, N_CTX_TOK, state[:, i].astype(F32), i, rp, DEC_BATCH, DEC_SEQ, 2)
    x = _out_proj(jnp.concatenate([att_c, att_l], axis=0), jnp.concatenate([rw_c, rw_l], axis=0), w_out, (i,),
                  x, gate, name="odd_out")
    new_ckv = c_kv[:N_CTX_TOK].reshape(BATCH, SEQ, KV_LORA)
    new_kpe = k_pe[:N_CTX_TOK].reshape(BATCH, SEQ, QK_ROPE)
    return x, new_ckv, new_kpe, s_c


def kernel(x_prompt, x_sample, c, c_ctx, cache_attn_k, cache_attn_v, cache_mla_ckv, cache_mla_kpe,
           state_rwkv, w_mod, b_mod, norm_mix, norm_ffn, norm_final, even_w_in, even_w_out, attn_sink,
           conv_w, odd_w_in, odd_w_out, mla_q_norm, mla_kv_norm, mla_w_uq, mla_w_ukv, rwkv_mu, rwkv_w0,
           rwkv_w_up, rwkv_a0, rwkv_a_up, rwkv_k_k, rwkv_k_a, rwkv_r_k, rwkv_g_up, rwkv_ln_w, rwkv_ln_b,
           router_group_w, router_group_b, router_expert_w, router_expert_b, expert_w_gate, expert_w_up,
           expert_w_down):
    d = D_MODEL
    x = jnp.concatenate([x_prompt.reshape(N_CTX_TOK, d), x_sample.reshape(N_LAT_TOK, d)], axis=0)
    cvec = jax.nn.silu(jnp.concatenate([c_ctx[None], c], axis=0))
    cvec = jnp.pad(cvec, ((0, 8 - N_MOD_GROUPS), (0, 0)))
    mp = dict(q_norm=mla_q_norm, kv_norm=mla_kv_norm, w_uq=mla_w_uq, w_ukv=mla_w_ukv)
    rp = dict(mu=rwkv_mu, w0=rwkv_w0, w_up=rwkv_w_up, a0=rwkv_a0, a_up=rwkv_a_up, k_k=rwkv_k_k,
              k_a=rwkv_k_a, r_k=rwkv_r_k, g_up=rwkv_g_up, ln_w=rwkv_ln_w, ln_b=rwkv_ln_b)
    odd_w_in_t = jnp.swapaxes(odd_w_in, 1, 2)
    new_k, new_v, new_ckv, new_kpe, new_s = [], [], [], [], []
    ffn = None
    for l in range(DEPTH):
        mod = _mm(cvec, w_mod, (l,), tn=1024, tk=2048, name="modulation")[:N_MOD_GROUPS] + b_mod[l]
        shift_mix, scale_mix, gate_mix, shift_ffn, scale_ffn, gate_ffn = jnp.split(mod, 6, axis=-1)
        if ffn is None:
            (h,) = _normmod(x, norm_mix[l], shift_mix, scale_mix, (BF16,))
        else:
            x, h = _normmod(x, norm_mix[l], shift_mix, scale_mix, (BF16,), res=ffn, emit_x=True)
        i = l // 2
        if l % 2 == 0:
            x, k_c, v_c = _even_mixer(h, i, cache_attn_k, cache_attn_v, even_w_in, even_w_out, attn_sink,
                                      conv_w, x, gate_mix)
            new_k.append(k_c)
            new_v.append(v_c)
        else:
            x, ckv_c, kpe_c, s_c = _odd_mixer(h, i, cache_mla_ckv, cache_mla_kpe, state_rwkv, odd_w_in_t,
                                              odd_w_out, mp, rp, x, gate_mix)
            new_ckv.append(ckv_c)
            new_kpe.append(kpe_c)
            new_s.append(s_c)
        w_router = jnp.pad(jnp.concatenate([router_group_w[l], router_expert_w[l]], axis=1),
                           ((0, 0), (0, LANES - N_GROUPS - N_EXPERTS)))
        b_router = jnp.concatenate([router_group_b[l], router_expert_b[l]])
        h_ffn, logits = _normmod(x, norm_ffn[l], shift_ffn, scale_ffn, (F32,), router=w_router)
        ffn = (_moe(l, h_ffn, logits, b_router, expert_w_gate, expert_w_up, expert_w_down), gate_ffn)
    zeros = jnp.zeros((N_MOD_GROUPS, d), F32)
    (y_all,) = _normmod(x, norm_final, zeros, zeros, (F32,), res=ffn)
    return (y_all[:N_CTX_TOK].reshape(BATCH, SEQ, d), y_all[N_CTX_TOK:].reshape(DEC_BATCH, DEC_SEQ, d),
            jnp.stack(new_k, axis=1), jnp.stack(new_v, axis=1), jnp.stack(new_ckv, axis=1),
            jnp.stack(new_kpe, axis=1), jnp.stack(new_s, axis=1))
```
